```python
import math
import jax, jax.numpy as jnp
from jax import lax
import numpy as np

D_MODEL = 1024
BATCH = 16
SEQ = 4096
DEPTH = 4
DEC_BATCH = 4
DEC_SEQ = 4096
PAST_LEN = 128

N_MIXERS = 2
HEAD_DIM = 64
N_HEADS_A = 16
DILATIONS = ((128, 1), (512, 4), (2048, 16))
N_DIL = len(DILATIONS)
DIL_BLOCK_Q = 64
N_BUCKETS = 32
T5_MAX_DISTANCE = 1024
N_HEADS_B = 16
GRID_W = 64
NA_KH = 8
NA_KW = 16
NA_COL_BLOCK = 16
NA_KEY_COLS = NA_COL_BLOCK + NA_KW
N_GROUPS = 4
EXPERTS_PER_GROUP = 8
N_EXPERTS = N_GROUPS * EXPERTS_PER_GROUP
TOP_K = 2
D_EXPERT = 512
ALPHA = (2 * DEPTH) ** 0.25
BETA = (8 * DEPTH) ** -0.25
N_A_LAYERS = (DEPTH + 1) // 2
N_B_LAYERS = DEPTH // 2
LN_EPS = 1e-5
NEG_INF = -1e30

kernel_name = "hybrid_dilated_neighbourhood_hmoe_encoder"


def layer_norm(x, g, b):
    xf = x.astype(jnp.float32)
    mu = jnp.mean(xf, axis=-1, keepdims=True)
    var = jnp.mean(jnp.square(xf - mu), axis=-1, keepdims=True)
    return ((xf - mu) * lax.rsqrt(var + LN_EPS) * g + b).astype(x.dtype)


def t5_bucket(rel):
    nb = N_BUCKETS // 2
    max_exact = nb // 2
    ret = jnp.where(rel > 0, nb, 0)
    n = jnp.abs(rel)
    nf = jnp.maximum(n, 1).astype(jnp.float32)
    large = max_exact + (jnp.log(nf / max_exact) / math.log(T5_MAX_DISTANCE / max_exact)
                         * (nb - max_exact)).astype(jnp.int32)
    large = jnp.minimum(large, nb - 1)
    return ret + jnp.where(n < max_exact, n, large)


def _dilated_branch(q, k, v, window, dilation, t5_table):
    B, S, H, Dh = q.shape
    half = window // (2 * dilation)
    L = S // dilation
    nblk = -(-L // DIL_BLOCK_Q)
    Lp = nblk * DIL_BLOCK_Q
    kw = DIL_BLOCK_Q + 2 * half

    def strided(a):
        return a.reshape(B, L, dilation, H, Dh).transpose(0, 2, 1, 3, 4)

    qs = jnp.pad(strided(q), ((0, 0), (0, 0), (0, Lp - L), (0, 0), (0, 0)))
    qs = qs.reshape(B, dilation, nblk, DIL_BLOCK_Q, H, Dh)
    pad_k = ((0, 0), (0, 0), (half, Lp - L + half), (0, 0), (0, 0))
    key_idx = np.arange(nblk)[:, None] * DIL_BLOCK_Q + np.arange(kw)[None, :]
    ks = jnp.take(jnp.pad(strided(k), pad_k), key_idx, axis=2)
    vs = jnp.take(jnp.pad(strided(v), pad_k), key_idx, axis=2)

    orig = key_idx - half
    valid = (orig >= 0) & (orig < L)
    rel = np.arange(kw)[None, :] - half - np.arange(DIL_BLOCK_Q)[:, None]
    band = np.abs(rel) <= half
    mask = band[None] & valid[:, None, :]
    bias = t5_table[t5_bucket(jnp.asarray(rel * dilation, dtype=jnp.int32))]
    bias = bias.transpose(2, 0, 1).astype(jnp.float32)

    s = jnp.einsum('bgnqhd,bgnkhd->bgnhqk', qs, ks).astype(jnp.float32) + bias
    s = jnp.where(mask[:, None], s, NEG_INF)
    lse = jax.nn.logsumexp(s, axis=-1)
    p = jnp.exp(s - lse[..., None]).astype(v.dtype)
    o = jnp.einsum('bgnhqk,bgnkhd->bgnqhd', p, vs)
    o = o.reshape(B, dilation, Lp, H, Dh)[:, :, :L].transpose(0, 2, 1, 3, 4).reshape(B, S, H, Dh)
    lse = lse.transpose(0, 1, 2, 4, 3).reshape(B, dilation, Lp, H)[:, :, :L]
    lse = lse.transpose(0, 2, 1, 3).reshape(B, S, H)
    return o, lse


def dilated_mixer(x, w_qkv, w_o, t5_table):
    B, S, _ = x.shape
    qkv = (x @ w_qkv).reshape(B, S, N_DIL, 3, N_HEADS_A, HEAD_DIM)
    outs, lses = [], []
    for g, (win, dil) in enumerate(DILATIONS):
        o, l = _dilated_branch(qkv[:, :, g, 0] * HEAD_DIM ** -0.5, qkv[:, :, g, 1],
                               qkv[:, :, g, 2], win, dil, t5_table)
        outs.append(o)
        lses.append(l)
    wts = jax.nn.softmax(jnp.stack(lses, 0), axis=0)
    o = jnp.einsum('gbsh,gbshd->bshd', wts, jnp.stack(outs, 0).astype(jnp.float32))
    return o.astype(x.dtype).reshape(B, S, N_HEADS_A * HEAD_DIM) @ w_o


def neighbourhood_mixer(x, w_qkv, w_o, rpb):
    B, S, _ = x.shape
    rows = S // GRID_W
    kh = min(NA_KH, rows)
    ncb = GRID_W // NA_COL_BLOCK
    H, Dh = N_HEADS_B, HEAD_DIM
    qkv = (x @ w_qkv).reshape(B, rows, GRID_W, 3, H, Dh)
    q = qkv[:, :, :, 0] * HEAD_DIM ** -0.5
    k = qkv[:, :, :, 1]
    v = qkv[:, :, :, 2]

    cols = np.arange(GRID_W)
    win_start = np.clip(cols - NA_KW // 2, 0, GRID_W - NA_KW)
    blk_start = np.clip(np.arange(ncb) * NA_COL_BLOCK - NA_KW // 2, 0, GRID_W - NA_KEY_COLS)
    key_cols = blk_start[:, None] + np.arange(NA_KEY_COLS)[None, :]
    q_cols = cols.reshape(ncb, NA_COL_BLOCK)
    ws = win_start[q_cols][:, :, None]
    kc = key_cols[:, None, :]
    col_mask = (kc >= ws) & (kc < ws + NA_KW)
    dc_idx = np.clip(kc - q_cols[:, :, None] + NA_KW - 1, 0, 2 * NA_KW - 2)

    k_cb = jnp.take(k, key_cols, axis=2)
    v_cb = jnp.take(v, key_cols, axis=2)

    def one_row(args):
        r, q_r = args
        rs = jnp.clip(r - kh // 2, 0, rows - kh)
        k_r = lax.dynamic_slice_in_dim(k_cb, rs, kh, axis=1)
        v_r = lax.dynamic_slice_in_dim(v_cb, rs, kh, axis=1)
        dr_idx = rs + jnp.arange(kh) - r + NA_KH - 1
        bias = jnp.take(jnp.take(rpb, dr_idx, axis=1), dc_idx, axis=2)
        bias = bias.transpose(2, 0, 3, 1, 4).astype(jnp.float32)
        s = jnp.einsum('bnqhd,binchd->bnhqic', q_r.reshape(B, ncb, NA_COL_BLOCK, H, Dh),
                       k_r).astype(jnp.float32) + bias
        s = jnp.where(col_mask[:, None, :, None, :], s, NEG_INF)
        p = jax.nn.softmax(s.reshape(B, ncb, H, NA_COL_BLOCK, kh * NA_KEY_COLS), axis=-1)
        p = p.reshape(s.shape).astype(v.dtype)
        o = jnp.einsum('bnhqic,binchd->bnqhd', p, v_r)
        return o.reshape(B, GRID_W, H, Dh)

    o = lax.map(one_row, (jnp.arange(rows), q.transpose(1, 0, 2, 3, 4)))
    o = o.transpose(1, 0, 2, 3, 4).reshape(B, S, H * Dh)
    return o @ w_o


def hierarchical_moe(x, w_rg, b_rg, w_re, b_re, w1, w3, w2):
    B, S, D = x.shape
    xf = x.reshape(-1, D)
    T = xf.shape[0]
    p_group = jax.nn.softmax((xf @ w_rg).astype(jnp.float32) + b_rg, axis=-1)
    g_sel = jnp.argmax(p_group, axis=-1)
    e_logits = ((xf @ w_re).astype(jnp.float32) + b_re).reshape(T, N_GROUPS, EXPERTS_PER_GROUP)
    e_in = jnp.take_along_axis(e_logits, g_sel[:, None, None], axis=1)[:, 0]
    top_v, top_i = lax.top_k(e_in, TOP_K)
    gate = jax.nn.softmax(top_v, axis=-1) * jnp.take_along_axis(p_group, g_sel[:, None], axis=1)
    eid = (g_sel[:, None] * EXPERTS_PER_GROUP + top_i).reshape(-1)
    tok = jnp.repeat(jnp.arange(T), TOP_K)
    order = jnp.argsort(eid)
    tok_s = tok[order]
    gate_s = gate.reshape(-1)[order]
    sizes = jnp.bincount(eid, length=N_EXPERTS).astype(jnp.int32)
    xs = xf[tok_s]
    h = jax.nn.silu(lax.ragged_dot(xs, w1, sizes)) * lax.ragged_dot(xs, w3, sizes)
    o = lax.ragged_dot(h, w2, sizes) * gate_s[:, None].astype(x.dtype)
    y = jnp.zeros_like(xf).at[tok_s].add(o)
    return y.reshape(B, S, D)


def _trunk(x, t5_table, w_qkv_a, w_o_a, w_qkv_b, w_o_b, rpb_b, ln_g, ln_b,
           w_rg, b_rg, w_re, b_re, w1, w3, w2):
    for i in range(DEPTH):
        j = i // N_MIXERS
        if i % N_MIXERS == 0:
            h = dilated_mixer(x, w_qkv_a[j], w_o_a[j], t5_table)
        else:
            h = neighbourhood_mixer(x, w_qkv_b[j], w_o_b[j], rpb_b[j])
        x = layer_norm(ALPHA * x + h, ln_g[i, 0], ln_b[i, 0])
        f = hierarchical_moe(x, w_rg[i], b_rg[i], w_re[i], b_re[i], w1[i], w3[i], w2[i])
        x = layer_norm(ALPHA * x + f, ln_g[i, 1], ln_b[i, 1])
    return x


def setup_inputs(seed: int = 0) -> dict:
    key = jax.random.key(seed)
    ks = jax.random.split(key, 17)
    f32 = jnp.float32

    def nrm(k, shape, scale):
        return jax.random.normal(k, shape, f32) * scale

    da = N_HEADS_A * HEAD_DIM
    db = N_HEADS_B * HEAD_DIM
    return {
        "x_prompt": nrm(ks[0], (BATCH, SEQ, D_MODEL), 1.0),
        "x_sample": nrm(ks[1], (DEC_BATCH, DEC_SEQ, D_MODEL), 1.0),
        "t5_table": nrm(ks[2], (N_BUCKETS, N_HEADS_A), 0.5),
        "w_qkv_a": nrm(ks[3], (N_A_LAYERS, D_MODEL, N_DIL * 3 * da), D_MODEL ** -0.5),
        "w_o_a": nrm(ks[4], (N_A_LAYERS, da, D_MODEL), BETA * da ** -0.5),
        "w_qkv_b": nrm(ks[5], (N_B_LAYERS, D_MODEL, 3 * db), D_MODEL ** -0.5),
        "w_o_b": nrm(ks[6], (N_B_LAYERS, db, D_MODEL), BETA * db ** -0.5),
        "rpb_b": nrm(ks[7], (N_B_LAYERS, N_HEADS_B, 2 * NA_KH - 1, 2 * NA_KW - 1), 0.5),
        "ln_g": 1.0 + nrm(ks[8], (DEPTH, 2, D_MODEL), 0.02),
        "ln_b": nrm(ks[9], (DEPTH, 2, D_MODEL), 0.02),
        "w_rg": nrm(ks[10], (DEPTH, D_MODEL, N_GROUPS), D_MODEL ** -0.5),
        "b_rg": nrm(ks[11], (DEPTH, N_GROUPS), 0.01),
        "w_re": nrm(ks[12], (DEPTH, D_MODEL, N_EXPERTS), D_MODEL ** -0.5),
        "b_re": nrm(ks[13], (DEPTH, N_EXPERTS), 0.01),
        "w1": nrm(ks[14], (DEPTH, N_EXPERTS, D_MODEL, D_EXPERT), D_MODEL ** -0.5),
        "w3": nrm(ks[15], (DEPTH, N_EXPERTS, D_MODEL, D_EXPERT), D_MODEL ** -0.5),
        "w2": nrm(ks[16], (DEPTH, N_EXPERTS, D_EXPERT, D_MODEL), BETA * D_EXPERT ** -0.5),
    }


def reference(x_prompt, x_sample, t5_table, w_qkv_a, w_o_a, w_qkv_b, w_o_b, rpb_b,
              ln_g, ln_b, w_rg, b_rg, w_re, b_re, w1, w3, w2):
    y_prompt = _trunk(x_prompt, t5_table, w_qkv_a, w_o_a, w_qkv_b, w_o_b, rpb_b, ln_g, ln_b,
                      w_rg, b_rg, w_re, b_re, w1, w3, w2)
    y_sample = _trunk(x_sample, t5_table, w_qkv_a, w_o_a, w_qkv_b, w_o_b, rpb_b, ln_g, ln_b,
                      w_rg, b_rg, w_re, b_re, w1, w3, w2)
    return (y_prompt, y_sample)
```

```python
import functools
import math

import numpy as np
import jax
import jax.numpy as jnp
from jax import lax
from jax.experimental import pallas as pl
from jax.experimental.pallas import tpu as pltpu

D_MODEL = 1024
SEQ = 4096
DEPTH = 4
HEAD_DIM = 64
N_HEADS = 16
DILATIONS = ((128, 1), (512, 4), (2048, 16))
N_BUCKETS = 32
T5_MAX_DISTANCE = 1024
GRID_W = 64
NA_KH = 8
NA_KW = 16
N_GROUPS = 4
EXPERTS_PER_GROUP = 8
N_EXPERTS = N_GROUPS * EXPERTS_PER_GROUP
D_EXPERT = 512
ALPHA = (2 * DEPTH) ** 0.25
LN_EPS = 1e-5
NEG_INF = -1e30

LANES = 128
HEAD_PAIRS = N_HEADS * HEAD_DIM // LANES
QKV_BLOCKS = 3 * HEAD_PAIRS
VMEM_LIMIT = 56 * 1024 * 1024

QBLK = 128
KBLK = 256
HALF = 64
PAIRS_PER_GROUP = EXPERTS_PER_GROUP * (EXPERTS_PER_GROUP - 1) // 2
N_CLASSES = N_GROUPS * PAIRS_PER_GROUP
ROW_W = D_MODEL + LANES
MOE_TM = 256
TOK_TM = 512

f32 = jnp.float32
bf16 = jnp.bfloat16


def _params(n_grid):
    return pltpu.CompilerParams(dimension_semantics=("arbitrary",) * n_grid,
                                vmem_limit_bytes=VMEM_LIMIT)


def _qkv_kernel(x_ref, w_ref, o_ref, acc_ref, *, dil, tm, cw):
    xb = x_ref[...].astype(bf16)
    nsub = cw // LANES
    rows = tm // dil
    for j in range(QKV_BLOCKS // nsub):
        res = jnp.dot(xb, w_ref[:, j * cw:(j + 1) * cw], preferred_element_type=f32)
        if dil == 1:
            for kk in range(nsub):
                o_ref[j * nsub + kk, 0, :, :] = res[:, kk * LANES:(kk + 1) * LANES].astype(bf16)
            continue
        for kk in range(nsub):
            acc_ref[kk] = res[:, kk * LANES:(kk + 1) * LANES]
        for kk in range(nsub):
            for r in range(dil):
                o_ref[j * nsub + kk, r, :, :] = acc_ref[kk, pl.ds(r, rows, stride=dil), :].astype(bf16)


def _qkv_proj(x3, w, dil):
    B, S, D = x3.shape
    tm, cw = 1024, 512
    L = S // dil
    return pl.pallas_call(
        functools.partial(_qkv_kernel, dil=dil, tm=tm, cw=cw),
        grid=(B, S // tm),
        in_specs=[pl.BlockSpec((None, tm, D), lambda b, i: (b, i, 0)),
                  pl.BlockSpec((D, QKV_BLOCKS * LANES), lambda b, i: (0, 0))],
        out_specs=pl.BlockSpec((None, QKV_BLOCKS, dil, tm // dil, LANES), lambda b, i: (b, 0, 0, i, 0)),
        out_shape=jax.ShapeDtypeStruct((B, QKV_BLOCKS, dil, L, LANES), bf16),
        scratch_shapes=[pltpu.VMEM((cw // LANES, tm, LANES), f32)],
        compiler_params=_params(2),
        name=f"qkv_proj_d{dil}",
    )(x3, w)


def _softmax_block(qst, kb, vb, bias):
    nq = qst.shape[0] // 2
    s = lax.dot_general(qst, kb, (((1,), (1,)), ((), ())), preferred_element_type=f32) + bias
    m = jnp.max(s, axis=1, keepdims=True)
    p = jnp.exp(s - m)
    l = jnp.sum(p, axis=1, keepdims=True)
    o2 = jnp.dot(p.astype(bf16), vb, preferred_element_type=f32) * (1.0 / l)
    lse = m + jnp.log(l)
    lane = lax.broadcasted_iota(jnp.int32, (nq, LANES), 1)
    o = jnp.where(lane < HEAD_DIM, o2[:nq], o2[nq:])
    ls = jnp.where(lane < HEAD_DIM, lse[:nq], lse[nq:])
    return o, ls


def _stack_heads(q2):
    lane = lax.broadcasted_iota(jnp.int32, q2.shape, 1)
    zero = jnp.zeros_like(q2)
    return jnp.concatenate([jnp.where(lane < HEAD_DIM, q2, zero),
                            jnp.where(lane >= HEAD_DIM, q2, zero)], axis=0)


def _attn_a_kernel(*refs):
    qkv = refs[:9]
    biases = refs[9:12]
    o_ref = refs[12]
    oscr, lscr = refs[13], refs[14]
    S = o_ref.shape[0]
    for g, (_, dil) in enumerate(DILATIONS):
        q_ref, k_ref, v_ref = qkv[3 * g:3 * g + 3]
        b_ref = biases[g]
        L = S // dil
        nb = L // QBLK

        def body(it, carry, q_ref=q_ref, k_ref=k_ref, v_ref=v_ref, b_ref=b_ref, dil=dil, L=L, nb=nb, g=g):
            r = it // nb
            n = it % nb
            qs = pl.multiple_of(n * QBLK, QBLK)
            ks = pl.multiple_of(jnp.clip(qs - HALF, 0, L - KBLK), HALF)
            var = jnp.where(n == 0, 0, jnp.where(n == nb - 1, 2, 1))
            qst = _stack_heads(q_ref[r, pl.ds(qs, QBLK), :])
            o, ls = _softmax_block(qst, k_ref[r, pl.ds(ks, KBLK), :], v_ref[r, pl.ds(ks, KBLK), :], b_ref[var])
            if dil == 1:
                dst = pl.ds(qs, QBLK)
            else:
                dst = pl.ds(qs * dil + r, QBLK, stride=dil)
            oscr[g, dst, :] = o
            lscr[g, dst, :] = ls
            return carry

        lax.fori_loop(0, dil * nb, body, 0)

    ch = 256

    def merge(c, carry):
        rows = pl.ds(pl.multiple_of(c * ch, ch), ch)
        l0, l1, l2 = lscr[0, rows, :], lscr[1, rows, :], lscr[2, rows, :]
        mx = jnp.maximum(jnp.maximum(l0, l1), l2)
        e0, e1, e2 = jnp.exp(l0 - mx), jnp.exp(l1 - mx), jnp.exp(l2 - mx)
        inv = 1.0 / (e0 + e1 + e2)
        o = (e0 * oscr[0, rows, :] + e1 * oscr[1, rows, :] + e2 * oscr[2, rows, :]) * inv
        o_ref[rows, :] = o.astype(bf16)
        return carry

    lax.fori_loop(0, S // ch, merge, 0)


def _attn_a(qkvs, biases):
    B = qkvs[0].shape[0]
    S = qkvs[0].shape[2] * qkvs[0].shape[3]
    in_specs, args = [], []
    for g, (_, dil) in enumerate(DILATIONS):
        L = S // dil
        for c in range(3):
            in_specs.append(pl.BlockSpec((None, None, dil, L, LANES),
                                         lambda b, hp, c=c: (b, c * HEAD_PAIRS + hp, 0, 0, 0)))
            args.append(qkvs[g])
    for g in range(3):
        in_specs.append(pl.BlockSpec((None, 3, 2 * QBLK, KBLK), lambda b, hp: (hp, 0, 0, 0)))
        args.append(biases[g])
    return pl.pallas_call(
        _attn_a_kernel,
        grid=(B, HEAD_PAIRS),
        in_specs=in_specs,
        out_specs=pl.BlockSpec((None, None, S, LANES), lambda b, hp: (b, hp, 0, 0)),
        out_shape=jax.ShapeDtypeStruct((B, HEAD_PAIRS, S, LANES), bf16),
        scratch_shapes=[pltpu.VMEM((3, S, LANES), f32), pltpu.VMEM((3, S, LANES), f32)],
        compiler_params=_params(2),
        name="dilated_attn",
    )(*args)


def _attn_b_kernel(q_ref, k_ref, v_ref, b_ref, o_ref):
    rows = o_ref.shape[0] // GRID_W
    kh = min(NA_KH, rows)
    nk = kh * GRID_W

    def body(i, carry):
        rs = jnp.clip(i - kh // 2, 0, rows - kh)
        var = rs - i + NA_KH - 1 - (NA_KH - kh)
        qs = pl.multiple_of(i * GRID_W, GRID_W)
        ks = pl.multiple_of(rs * GRID_W, GRID_W)
        qst = _stack_heads(q_ref[pl.ds(qs, GRID_W), :])
        o, _ = _softmax_block(qst, k_ref[pl.ds(ks, nk), :], v_ref[pl.ds(ks, nk), :], b_ref[var])
        o_ref[pl.ds(qs, GRID_W), :] = o.astype(bf16)
        return carry

    lax.fori_loop(0, rows, body, 0)


def _attn_b(qkv, bias):
    B, _, _, S, _ = qkv.shape
    nvar, nk = bias.shape[1], bias.shape[3]
    in_specs = [pl.BlockSpec((None, None, None, S, LANES), lambda b, hp, c=c: (b, c * HEAD_PAIRS + hp, 0, 0, 0))
                for c in range(3)]
    in_specs.append(pl.BlockSpec((None, nvar, 2 * GRID_W, nk), lambda b, hp: (hp, 0, 0, 0)))
    return pl.pallas_call(
        _attn_b_kernel,
        grid=(B, HEAD_PAIRS),
        in_specs=in_specs,
        out_specs=pl.BlockSpec((None, None, S, LANES), lambda b, hp: (b, hp, 0, 0)),
        out_shape=jax.ShapeDtypeStruct((B, HEAD_PAIRS, S, LANES), bf16),
        compiler_params=_params(2),
        name="neighbourhood_attn",
    )(qkv, qkv, qkv, bias)


def _layer_norm(y, g, b):
    mu = jnp.mean(y, axis=1, keepdims=True)
    yc = y - mu
    var = jnp.mean(yc * yc, axis=1, keepdims=True)
    return yc * lax.rsqrt(var + LN_EPS) * g + b


def _route(logits):
    shape = logits.shape
    lane = lax.broadcasted_iota(jnp.int32, shape, 1)
    big = jnp.int32(4 * LANES)
    is_g = lane < N_GROUPS
    gl = jnp.where(is_g, logits, NEG_INF)
    mg = jnp.max(gl, axis=1, keepdims=True)
    gsel = jnp.min(jnp.where(gl == mg, lane, big), axis=1, keepdims=True)
    p_sel = 1.0 / jnp.sum(jnp.where(is_g, jnp.exp(gl - mg), 0.0), axis=1, keepdims=True)
    lo = N_GROUPS + gsel * EXPERTS_PER_GROUP
    el = jnp.where((lane >= lo) & (lane < lo + EXPERTS_PER_GROUP), logits, NEG_INF)
    v1 = jnp.max(el, axis=1, keepdims=True)
    i1 = jnp.min(jnp.where(el == v1, lane, big), axis=1, keepdims=True)
    el2 = jnp.where(lane == i1, NEG_INF, el)
    v2 = jnp.max(el2, axis=1, keepdims=True)
    i2 = jnp.min(jnp.where(el2 == v2, lane, big), axis=1, keepdims=True)
    t = jnp.exp(v2 - v1)
    g1 = p_sel / (1.0 + t)
    g2 = p_sel * t / (1.0 + t)
    first_lo = i1 < i2
    a = jnp.where(first_lo, i1, i2) - lo
    b = jnp.where(first_lo, i2, i1) - lo
    ga = jnp.where(first_lo, g1, g2)
    gb = jnp.where(first_lo, g2, g1)
    pair = (EXPERTS_PER_GROUP - 1) * a - ((a * (a - 1)) >> 1) + (b - a - 1)
    cls = (gsel * PAIRS_PER_GROUP + pair).astype(f32)
    return jnp.where(lane == 0, ga, jnp.where(lane == 1, gb, jnp.where(lane == 2, cls, 0.0)))


def _oproj_kernel(att_ref, x_ref, wo_ref, g_ref, b_ref, wr_ref, br_ref, o_ref):
    a = jnp.concatenate([att_ref[h] for h in range(HEAD_PAIRS)], axis=1)
    h = jnp.dot(a, wo_ref[...], preferred_element_type=f32)
    xn = _layer_norm(ALPHA * x_ref[...] + h, g_ref[...], b_ref[...])
    logits = jnp.dot(xn, wr_ref[...], preferred_element_type=f32,
                     precision=lax.Precision.HIGHEST) + br_ref[...]
    o_ref[:, :D_MODEL] = xn
    o_ref[:, D_MODEL:] = _route(logits)


def _oproj_ln_route(att, x, wo, g, b, wr, br):
    B, _, S, _ = att.shape
    T = B * S
    tm = TOK_TM
    per = S // tm
    vec = lambda n: pl.BlockSpec((1, n), lambda i: (0, 0))
    return pl.pallas_call(
        _oproj_kernel,
        grid=(T // tm,),
        in_specs=[pl.BlockSpec((None, HEAD_PAIRS, tm, LANES), lambda i: (i // per, 0, i % per, 0)),
                  pl.BlockSpec((tm, D_MODEL), lambda i: (i, 0)),
                  pl.BlockSpec((D_MODEL, D_MODEL), lambda i: (0, 0)),
                  vec(D_MODEL), vec(D_MODEL),
                  pl.BlockSpec((D_MODEL, LANES), lambda i: (0, 0)), vec(LANES)],
        out_specs=pl.BlockSpec((tm, ROW_W), lambda i: (i, 0)),
        out_shape=jax.ShapeDtypeStruct((T, ROW_W), f32),
        compiler_params=_params(1),
        name="oproj_ln_route",
    )(att, x, wo, g, b, wr, br)


def _dispatch_kernel(pos_ref, rows_ref, init_ref, xs_ref, sem):
    del init_ref
    tm = rows_ref.shape[0]

    def copy(k):
        return pltpu.make_async_copy(rows_ref.at[pl.ds(k, 1), :], xs_ref.at[pl.ds(pos_ref[0, 0, k], 1), :], sem)

    def start(k, c):
        copy(k).start()
        return c

    def wait(k, c):
        copy(k).wait()
        return c

    lax.fori_loop(0, tm, start, 0)
    lax.fori_loop(0, tm, wait, 0)


def _dispatch(rows, pos3, n_slots):
    T = rows.shape[0]
    tm = pos3.shape[2]
    init = jnp.zeros((n_slots, ROW_W), f32)
    return pl.pallas_call(
        _dispatch_kernel,
        grid=(T // tm,),
        in_specs=[pl.BlockSpec((1, 1, tm), lambda i: (i, 0, 0), memory_space=pltpu.SMEM),
                  pl.BlockSpec((tm, ROW_W), lambda i: (i, 0)),
                  pl.BlockSpec(memory_space=pl.ANY)],
        out_specs=pl.BlockSpec(memory_space=pl.ANY),
        out_shape=jax.ShapeDtypeStruct((n_slots, ROW_W), f32),
        scratch_shapes=[pltpu.SemaphoreType.DMA(())],
        input_output_aliases={2: 0},
        compiler_params=_params(1),
        name="moe_dispatch",
    )(pos3, rows, init)


def _swiglu(xt, w1, w3, w2):
    h1 = jnp.dot(xt, w1, preferred_element_type=f32)
    h3 = jnp.dot(xt, w3, preferred_element_type=f32)
    h = h1 * (1.0 / (1.0 + jnp.exp(-h1))) * h3
    return jnp.dot(h.astype(bf16), w2, preferred_element_type=f32)


def _moe_kernel(ea_ref, eb_ref, valid_ref, xs_ref, w1a, w3a, w2a, w1b, w3b, w2b, ys_ref):
    i = pl.program_id(0)

    @pl.when(valid_ref[i] != 0)
    def _():
        xt = xs_ref[:, :D_MODEL].astype(bf16)
        ga = xs_ref[:, D_MODEL:D_MODEL + 1]
        gb = xs_ref[:, D_MODEL + 1:D_MODEL + 2]
        ys_ref[...] = (ga * _swiglu(xt, w1a[...], w3a[...], w2a[...])
                       + gb * _swiglu(xt, w1b[...], w3b[...], w2b[...]))

    @pl.when(valid_ref[i] == 0)
    def _():
        ys_ref[...] = jnp.zeros_like(ys_ref)


def _moe(xs, tile_a, tile_b, tile_valid, w1, w3, w2, layer):
    n_slots = xs.shape[0]
    tm = MOE_TM
    wa = lambda shape: pl.BlockSpec((None, None) + shape, lambda i, ea, eb, va: (layer, ea[i], 0, 0))
    wb = lambda shape: pl.BlockSpec((None, None) + shape, lambda i, ea, eb, va: (layer, eb[i], 0, 0))
    up, down = (D_MODEL, D_EXPERT), (D_EXPERT, D_MODEL)
    grid_spec = pltpu.PrefetchScalarGridSpec(
        num_scalar_prefetch=3,
        grid=(n_slots // tm,),
        in_specs=[pl.BlockSpec((tm, ROW_W), lambda i, ea, eb, va: (i, 0)),
                  wa(up), wa(up), wa(down), wb(up), wb(up), wb(down)],
        out_specs=pl.BlockSpec((tm, D_MODEL), lambda i, ea, eb, va: (i, 0)),
    )
    return pl.pallas_call(
        _moe_kernel,
        grid_spec=grid_spec,
        out_shape=jax.ShapeDtypeStruct((n_slots, D_MODEL), f32),
        compiler_params=_params(1),
        name="moe_experts",
    )(tile_a, tile_b, tile_valid, xs, w1, w3, w2, w1, w3, w2)


def _combine_kernel(pos_ref, x_ref, ys_ref, g_ref, b_ref, o_ref, buf, sem):
    tm = x_ref.shape[0]

    def copy(k):
        return pltpu.make_async_copy(ys_ref.at[pl.ds(pos_ref[0, 0, k], 1), :], buf.at[pl.ds(k, 1), :], sem)

    def start(k, c):
        copy(k).start()
        return c

    def wait(k, c):
        copy(k).wait()
        return c

    lax.fori_loop(0, tm, start, 0)
    lax.fori_loop(0, tm, wait, 0)
    o_ref[...] = _layer_norm(ALPHA * x_ref[...] + buf[...], g_ref[...], b_ref[...])


def _combine_ln(rows, ys, pos3, g, b):
    T = rows.shape[0]
    tm = pos3.shape[2]
    vec = pl.BlockSpec((1, D_MODEL), lambda i: (0, 0))
    return pl.pallas_call(
        _combine_kernel,
        grid=(T // tm,),
        in_specs=[pl.BlockSpec((1, 1, tm), lambda i: (i, 0, 0), memory_space=pltpu.SMEM),
                  pl.BlockSpec((tm, D_MODEL), lambda i: (i, 0)),
                  pl.BlockSpec(memory_space=pl.ANY), vec, vec],
        out_specs=pl.BlockSpec((tm, D_MODEL), lambda i: (i, 0)),
        out_shape=jax.ShapeDtypeStruct((T, D_MODEL), f32),
        scratch_shapes=[pltpu.VMEM((tm, D_MODEL), f32), pltpu.SemaphoreType.DMA(())],
        compiler_params=_params(1),
        name="moe_combine_ln",
    )(pos3, rows, ys, g, b)


def _plan(cls, n_tiles):
    onehot = (cls[:, None] == jnp.arange(N_CLASSES, dtype=jnp.int32)[None, :]).astype(jnp.int32)
    csum = jnp.cumsum(onehot, axis=0)
    rank = jnp.sum(csum * onehot, axis=1) - 1
    counts = csum[-1]
    tiles = (counts + MOE_TM - 1) // MOE_TM
    tile_end = jnp.cumsum(tiles)
    tile_start = tile_end - tiles
    pos = tile_start[cls] * MOE_TM + rank
    tile_id = jnp.arange(n_tiles, dtype=jnp.int32)
    used = tile_end[-1]
    tcls = jnp.searchsorted(tile_end, jnp.minimum(tile_id, used - 1), side="right").astype(jnp.int32)
    grp = tcls // PAIRS_PER_GROUP
    pair = tcls % PAIRS_PER_GROUP
    pa, pb = np.triu_indices(EXPERTS_PER_GROUP, 1)
    ea = grp * EXPERTS_PER_GROUP + jnp.asarray(pa, jnp.int32)[pair]
    eb = grp * EXPERTS_PER_GROUP + jnp.asarray(pb, jnp.int32)[pair]
    valid = (tile_id < used).astype(jnp.int32)
    return pos.astype(jnp.int32), ea, eb, valid


def _t5_buckets(rel):
    nb = N_BUCKETS // 2
    max_exact = nb // 2
    ret = np.where(rel > 0, nb, 0)
    n = np.abs(rel)
    nf = np.maximum(n, 1).astype(np.float32)
    large = max_exact + (np.log(nf / np.float32(max_exact)) / np.float32(math.log(T5_MAX_DISTANCE / max_exact))
                         * np.float32(nb - max_exact)).astype(np.int32)
    large = np.minimum(large, nb - 1)
    return ret + np.where(n < max_exact, n, large)


def _dilated_bias(t5_table, dil):
    qi = np.arange(QBLK)[:, None]
    kj = np.arange(KBLK)[None, :]
    out = []
    for shift in (0, HALF, 2 * HALF):
        rel = kj - qi - shift
        band = np.abs(rel) <= HALF
        vals = t5_table[jnp.asarray(_t5_buckets(rel * dil), jnp.int32)]
        vals = jnp.where(jnp.asarray(band)[:, :, None], vals.astype(f32), NEG_INF)
        out.append(vals.transpose(2, 0, 1))
    tab = jnp.stack(out, axis=1)
    tab = tab.reshape(HEAD_PAIRS, 2, 3, QBLK, KBLK).transpose(0, 2, 1, 3, 4)
    return tab.reshape(HEAD_PAIRS, 3, 2 * QBLK, KBLK)


def _neighbourhood_bias(rpb, rows):
    kh = min(NA_KH, rows)
    nvar = NA_KH if rows > kh else 1
    j = np.arange(GRID_W)[:, None]
    kc = np.arange(GRID_W)[None, :]
    ws = np.clip(j - NA_KW // 2, 0, GRID_W - NA_KW)
    inwin = (kc >= ws) & (kc < ws + NA_KW)
    dc = np.clip(kc - j + NA_KW - 1, 0, 2 * NA_KW - 2)
    out = []
    for var in range(nvar):
        dr = var + (NA_KH - kh) + np.arange(kh)
        vals = rpb[:, dr][:, :, dc]
        vals = jnp.where(jnp.asarray(inwin)[None, None], vals.astype(f32), NEG_INF)
        out.append(vals.transpose(0, 2, 1, 3).reshape(N_HEADS, GRID_W, kh * GRID_W))
    tab = jnp.stack(out, axis=1)
    tab = tab.reshape(HEAD_PAIRS, 2, nvar, GRID_W, kh * GRID_W).transpose(0, 2, 1, 3, 4)
    return tab.reshape(HEAD_PAIRS, nvar, 2 * GRID_W, kh * GRID_W)


def _qkv_weight(w):
    hd = N_HEADS * HEAD_DIM
    scale = jnp.concatenate([jnp.full((hd,), HEAD_DIM ** -0.5, f32), jnp.ones((2 * hd,), f32)])
    return (w * scale[None, :]).astype(bf16)


def _router_weight(w_rg, b_rg, w_re, b_re):
    pad = LANES - N_GROUPS - N_EXPERTS
    wr = jnp.concatenate([w_rg, w_re, jnp.zeros((D_MODEL, pad), f32)], axis=1)
    br = jnp.concatenate([b_rg, b_re, jnp.zeros((pad,), f32)])[None, :]
    return wr, br


def _trunk(x3, t5_table, w_qkv_a, w_o_a, w_qkv_b, w_o_b, rpb_b, ln_g, ln_b,
           w_rg, b_rg, w_re, b_re, w1, w3, w2):
    B, S, D = x3.shape
    T = B * S
    n_slots = T + N_CLASSES * MOE_TM
    n_tiles = n_slots // MOE_TM
    hd = N_HEADS * HEAD_DIM
    x = x3.reshape(T, D)
    w1b, w3b, w2b = w1.astype(bf16), w3.astype(bf16), w2.astype(bf16)
    bias_a = [_dilated_bias(t5_table, dil) for _, dil in DILATIONS]
    for i in range(DEPTH):
        j = i // 2
        x3 = x.reshape(B, S, D)
        if i % 2 == 0:
            qkvs = [_qkv_proj(x3, _qkv_weight(w_qkv_a[j][:, g * 3 * hd:(g + 1) * 3 * hd]), dil)
                    for g, (_, dil) in enumerate(DILATIONS)]
            att = _attn_a(qkvs, bias_a)
            wo = w_o_a[j]
        else:
            qkv = _qkv_proj(x3, _qkv_weight(w_qkv_b[j]), 1)
            att = _attn_b(qkv, _neighbourhood_bias(rpb_b[j], S // GRID_W))
            wo = w_o_b[j]
        wr, br = _router_weight(w_rg[i], b_rg[i], w_re[i], b_re[i])
        rows = _oproj_ln_route(att, x, wo.astype(bf16), ln_g[i, 0][None, :], ln_b[i, 0][None, :], wr, br)
        cls = rows[:, D_MODEL + 2].astype(jnp.int32)
        pos, ea, eb, valid = _plan(cls, n_tiles)
        pos3 = pos.reshape(T // TOK_TM, 1, TOK_TM)
        xs = _dispatch(rows, pos3, n_slots)
        ys = _moe(xs, ea, eb, valid, w1b, w3b, w2b, i)
        x = _combine_ln(rows, ys, pos3, ln_g[i, 1][None, :], ln_b[i, 1][None, :])
    return x.reshape(B, S, D)


def kernel(x_prompt, x_sample, t5_table, w_qkv_a, w_o_a, w_qkv_b, w_o_b, rpb_b, ln_g, ln_b,
           w_rg, b_rg, w_re, b_re, w1, w3, w2):
    nb = x_prompt.shape[0]
    x = jnp.concatenate([x_prompt, x_sample], axis=0)
    y = _trunk(x, t5_table, w_qkv_a, w_o_a, w_qkv_b, w_o_b, rpb_b, ln_g, ln_b,
               w_rg, b_rg, w_re, b_re, w1, w3, w2)
    return (y[:nb], y[nb:])
```

```python
import functools
import math

import numpy as np
import jax
import jax.numpy as jnp
from jax import lax
from jax.experimental import pallas as pl
from jax.experimental.pallas import tpu as pltpu

D_MODEL = 1024
SEQ = 4096
DEPTH = 4
HEAD_DIM = 64
N_HEADS = 16
DILATIONS = ((128, 1), (512, 4), (2048, 16))
N_BUCKETS = 32
T5_MAX_DISTANCE = 1024
GRID_W = 64
NA_KH = 8
NA_KW = 16
N_GROUPS = 4
EXPERTS_PER_GROUP = 8
N_EXPERTS = N_GROUPS * EXPERTS_PER_GROUP
D_EXPERT = 512
ALPHA = (2 * DEPTH) ** 0.25
LN_EPS = 1e-5
NEG_INF = -1e30

LANES = 128
HEAD_PAIRS = N_HEADS * HEAD_DIM // LANES
QKV_BLOCKS = 3 * HEAD_PAIRS
VMEM_LIMIT = 56 * 1024 * 1024

QBLK = 128
KBLK = 256
HALF = 64
NA_ROWS_PER_ITER = 4
DIL_BLOCKS_PER_ITER = 2
PAIRS_PER_GROUP = EXPERTS_PER_GROUP * (EXPERTS_PER_GROUP - 1) // 2
N_CLASSES = N_GROUPS * PAIRS_PER_GROUP
ROW_W = D_MODEL + LANES
MOE_TM = 256
TOK_TM = 512
DMA_UNROLL = 8

f32 = jnp.float32
bf16 = jnp.bfloat16


def _params(n_grid):
    return pltpu.CompilerParams(dimension_semantics=("arbitrary",) * n_grid,
                                vmem_limit_bytes=VMEM_LIMIT)


def _qkv_kernel(x_ref, w_ref, o_ref, acc_ref, *, dil, tm, cw):
    xb = x_ref[...].astype(bf16)
    nsub = cw // LANES
    rows = tm // dil
    for j in range(QKV_BLOCKS // nsub):
        res = jnp.dot(xb, w_ref[:, j * cw:(j + 1) * cw], preferred_element_type=f32)
        if dil == 1:
            for kk in range(nsub):
                o_ref[j * nsub + kk, 0, :, :] = res[:, kk * LANES:(kk + 1) * LANES].astype(bf16)
            continue
        for kk in range(nsub):
            acc_ref[kk] = res[:, kk * LANES:(kk + 1) * LANES]
        for kk in range(nsub):
            for r in range(dil):
                o_ref[j * nsub + kk, r, :, :] = acc_ref[kk, pl.ds(r, rows, stride=dil), :].astype(bf16)


def _qkv_proj(x3, w, dil):
    B, S, D = x3.shape
    tm, cw = 1024, 512
    L = S // dil
    return pl.pallas_call(
        functools.partial(_qkv_kernel, dil=dil, tm=tm, cw=cw),
        grid=(B, S // tm),
        in_specs=[pl.BlockSpec((None, tm, D), lambda b, i: (b, i, 0)),
                  pl.BlockSpec((D, QKV_BLOCKS * LANES), lambda b, i: (0, 0))],
        out_specs=pl.BlockSpec((None, QKV_BLOCKS, dil, tm // dil, LANES), lambda b, i: (b, 0, 0, i, 0)),
        out_shape=jax.ShapeDtypeStruct((B, QKV_BLOCKS, dil, L, LANES), bf16),
        scratch_shapes=[pltpu.VMEM((cw // LANES, tm, LANES), f32)],
        compiler_params=_params(2),
        name=f"qkv_proj_d{dil}",
    )(x3, w)


def _softmax_block(qst, kb, vb, bias):
    nq = qst.shape[0] // 2
    s = lax.dot_general(qst, kb, (((1,), (1,)), ((), ())), preferred_element_type=f32) + bias
    m = jnp.max(s, axis=1, keepdims=True)
    p = jnp.exp(s - m)
    l = jnp.sum(p, axis=1, keepdims=True)
    o2 = jnp.dot(p.astype(bf16), vb, preferred_element_type=f32) * (1.0 / l)
    lse = m + jnp.log(l)
    lane = lax.broadcasted_iota(jnp.int32, (nq, LANES), 1)
    o = jnp.where(lane < HEAD_DIM, o2[:nq], o2[nq:])
    ls = jnp.where(lane < HEAD_DIM, lse[:nq], lse[nq:])
    return o, ls


def _stack_heads(q2):
    lane = lax.broadcasted_iota(jnp.int32, q2.shape, 1)
    zero = jnp.zeros_like(q2)
    return jnp.concatenate([jnp.where(lane < HEAD_DIM, q2, zero),
                            jnp.where(lane >= HEAD_DIM, q2, zero)], axis=0)


def _attn_a_kernel(*refs):
    qkv = refs[:9]
    biases = refs[9:12]
    o_ref = refs[12]
    oscr, lscr = refs[13], refs[14]
    S = o_ref.shape[0]

    def block(it, g):
        dil = DILATIONS[g][1]
        q_ref, k_ref, v_ref = qkv[3 * g:3 * g + 3]
        L = S // dil
        nb = L // QBLK
        r = it // nb
        n = it % nb
        qs = pl.multiple_of(n * QBLK, QBLK)
        ks = pl.multiple_of(jnp.clip(qs - HALF, 0, L - KBLK), HALF)
        var = jnp.where(n == 0, 0, jnp.where(n == nb - 1, 2, 1))
        qst = _stack_heads(q_ref[r, pl.ds(qs, QBLK), :])
        o, ls = _softmax_block(qst, k_ref[r, pl.ds(ks, KBLK), :], v_ref[r, pl.ds(ks, KBLK), :], biases[g][var])
        if dil == 1:
            dst = pl.ds(qs, QBLK)
        else:
            dst = pl.ds(qs * dil + r, QBLK, stride=dil)
        oscr[g, dst, :] = o
        lscr[g, dst, :] = ls

    def body(it, carry):
        for u in range(DIL_BLOCKS_PER_ITER):
            for g in range(len(DILATIONS)):
                block(it * DIL_BLOCKS_PER_ITER + u, g)
        return carry

    lax.fori_loop(0, S // QBLK // DIL_BLOCKS_PER_ITER, body, 0)

    ch = 256

    def merge(c, carry):
        rows = pl.ds(pl.multiple_of(c * ch, ch), ch)
        l0, l1, l2 = lscr[0, rows, :], lscr[1, rows, :], lscr[2, rows, :]
        mx = jnp.maximum(jnp.maximum(l0, l1), l2)
        e0, e1, e2 = jnp.exp(l0 - mx), jnp.exp(l1 - mx), jnp.exp(l2 - mx)
        inv = 1.0 / (e0 + e1 + e2)
        o = (e0 * oscr[0, rows, :] + e1 * oscr[1, rows, :] + e2 * oscr[2, rows, :]) * inv
        o_ref[rows, :] = o.astype(bf16)
        return carry

    lax.fori_loop(0, S // ch, merge, 0)


def _attn_a(qkvs, biases):
    B = qkvs[0].shape[0]
    S = qkvs[0].shape[2] * qkvs[0].shape[3]
    in_specs, args = [], []
    for g, (_, dil) in enumerate(DILATIONS):
        L = S // dil
        for c in range(3):
            in_specs.append(pl.BlockSpec((None, None, dil, L, LANES),
                                         lambda b, hp, c=c: (b, c * HEAD_PAIRS + hp, 0, 0, 0)))
            args.append(qkvs[g])
    for g in range(3):
        in_specs.append(pl.BlockSpec((None, 3, 2 * QBLK, KBLK), lambda b, hp: (hp, 0, 0, 0)))
        args.append(biases[g])
    return pl.pallas_call(
        _attn_a_kernel,
        grid=(B, HEAD_PAIRS),
        in_specs=in_specs,
        out_specs=pl.BlockSpec((None, None, S, LANES), lambda b, hp: (b, hp, 0, 0)),
        out_shape=jax.ShapeDtypeStruct((B, HEAD_PAIRS, S, LANES), bf16),
        scratch_shapes=[pltpu.VMEM((3, S, LANES), f32), pltpu.VMEM((3, S, LANES), f32)],
        compiler_params=_params(2),
        name="dilated_attn",
    )(*args)


def _attn_b_kernel(q_ref, k_ref, v_ref, b_ref, o_ref):
    rows = o_ref.shape[0] // GRID_W
    kh = min(NA_KH, rows)
    nk = kh * GRID_W

    def one_row(i):
        rs = jnp.clip(i - kh // 2, 0, rows - kh)
        var = rs - i + NA_KH - 1 - (NA_KH - kh)
        qs = pl.multiple_of(i * GRID_W, GRID_W)
        ks = pl.multiple_of(rs * GRID_W, GRID_W)
        qst = _stack_heads(q_ref[pl.ds(qs, GRID_W), :])
        o, _ = _softmax_block(qst, k_ref[pl.ds(ks, nk), :], v_ref[pl.ds(ks, nk), :], b_ref[var])
        o_ref[pl.ds(qs, GRID_W), :] = o.astype(bf16)

    def body(it, carry):
        for u in range(NA_ROWS_PER_ITER):
            one_row(it * NA_ROWS_PER_ITER + u)
        return carry

    lax.fori_loop(0, rows // NA_ROWS_PER_ITER, body, 0)


def _attn_b(qkv, bias):
    B, _, _, S, _ = qkv.shape
    nvar, nk = bias.shape[1], bias.shape[3]
    in_specs = [pl.BlockSpec((None, None, None, S, LANES), lambda b, hp, c=c: (b, c * HEAD_PAIRS + hp, 0, 0, 0))
                for c in range(3)]
    in_specs.append(pl.BlockSpec((None, nvar, 2 * GRID_W, nk), lambda b, hp: (hp, 0, 0, 0)))
    return pl.pallas_call(
        _attn_b_kernel,
        grid=(B, HEAD_PAIRS),
        in_specs=in_specs,
        out_specs=pl.BlockSpec((None, None, S, LANES), lambda b, hp: (b, hp, 0, 0)),
        out_shape=jax.ShapeDtypeStruct((B, HEAD_PAIRS, S, LANES), bf16),
        compiler_params=_params(2),
        name="neighbourhood_attn",
    )(qkv, qkv, qkv, bias)


def _layer_norm(y, g, b):
    mu = jnp.mean(y, axis=1, keepdims=True)
    yc = y - mu
    var = jnp.mean(yc * yc, axis=1, keepdims=True)
    return yc * lax.rsqrt(var + LN_EPS) * g + b


def _route(logits):
    shape = logits.shape
    lane = lax.broadcasted_iota(jnp.int32, shape, 1)
    big = jnp.int32(4 * LANES)
    is_g = lane < N_GROUPS
    gl = jnp.where(is_g, logits, NEG_INF)
    mg = jnp.max(gl, axis=1, keepdims=True)
    gsel = jnp.min(jnp.where(gl == mg, lane, big), axis=1, keepdims=True)
    p_sel = 1.0 / jnp.sum(jnp.where(is_g, jnp.exp(gl - mg), 0.0), axis=1, keepdims=True)
    lo = N_GROUPS + gsel * EXPERTS_PER_GROUP
    el = jnp.where((lane >= lo) & (lane < lo + EXPERTS_PER_GROUP), logits, NEG_INF)
    v1 = jnp.max(el, axis=1, keepdims=True)
    i1 = jnp.min(jnp.where(el == v1, lane, big), axis=1, keepdims=True)
    el2 = jnp.where(lane == i1, NEG_INF, el)
    v2 = jnp.max(el2, axis=1, keepdims=True)
    i2 = jnp.min(jnp.where(el2 == v2, lane, big), axis=1, keepdims=True)
    t = jnp.exp(v2 - v1)
    g1 = p_sel / (1.0 + t)
    g2 = p_sel * t / (1.0 + t)
    first_lo = i1 < i2
    a = jnp.where(first_lo, i1, i2) - lo
    b = jnp.where(first_lo, i2, i1) - lo
    ga = jnp.where(first_lo, g1, g2)
    gb = jnp.where(first_lo, g2, g1)
    pair = (EXPERTS_PER_GROUP - 1) * a - ((a * (a - 1)) >> 1) + (b - a - 1)
    cls = (gsel * PAIRS_PER_GROUP + pair).astype(f32)
    return jnp.where(lane == 0, ga, jnp.where(lane == 1, gb, jnp.where(lane == 2, cls, 0.0)))


def _oproj_kernel(att_ref, x_ref, wo_ref, g_ref, b_ref, wr_ref, br_ref, o_ref):
    a = jnp.concatenate([att_ref[h] for h in range(HEAD_PAIRS)], axis=1)
    h = jnp.dot(a, wo_ref[...], preferred_element_type=f32)
    xn = _layer_norm(ALPHA * x_ref[...] + h, g_ref[...], b_ref[...])
    xh = xn.astype(bf16)
    xl = (xn - xh.astype(f32)).astype(bf16)
    hw = jnp.dot(xh, wr_ref[...], preferred_element_type=f32)
    lw = jnp.dot(xl, wr_ref[:, :LANES], preferred_element_type=f32)
    logits = hw[:, :LANES] + hw[:, LANES:] + lw + br_ref[...]
    o_ref[:, :D_MODEL] = xn
    o_ref[:, D_MODEL:] = _route(logits)


def _oproj_ln_route(att, x, wo, g, b, wr, br):
    B, _, S, _ = att.shape
    T = B * S
    tm = TOK_TM
    per = S // tm
    vec = lambda n: pl.BlockSpec((1, n), lambda i: (0, 0))
    return pl.pallas_call(
        _oproj_kernel,
        grid=(T // tm,),
        in_specs=[pl.BlockSpec((None, HEAD_PAIRS, tm, LANES), lambda i: (i // per, 0, i % per, 0)),
                  pl.BlockSpec((tm, D_MODEL), lambda i: (i, 0)),
                  pl.BlockSpec((D_MODEL, D_MODEL), lambda i: (0, 0)),
                  vec(D_MODEL), vec(D_MODEL),
                  pl.BlockSpec((D_MODEL, 2 * LANES), lambda i: (0, 0)), vec(LANES)],
        out_specs=pl.BlockSpec((tm, ROW_W), lambda i: (i, 0)),
        out_shape=jax.ShapeDtypeStruct((T, ROW_W), f32),
        compiler_params=_params(1),
        name="oproj_ln_route",
    )(att, x, wo, g, b, wr, br)


def _run_row_copies(copy, n):
    def start(c, carry):
        for u in range(DMA_UNROLL):
            copy(c * DMA_UNROLL + u).start()
        return carry

    def wait(c, carry):
        for u in range(DMA_UNROLL):
            copy(c * DMA_UNROLL + u).wait()
        return carry

    lax.fori_loop(0, n // DMA_UNROLL, start, 0)
    lax.fori_loop(0, n // DMA_UNROLL, wait, 0)


def _dispatch_kernel(pos_ref, rows_ref, init_ref, xs_ref, sem):
    del init_ref

    def copy(k):
        return pltpu.make_async_copy(rows_ref.at[pl.ds(k, 1), :], xs_ref.at[pl.ds(pos_ref[0, 0, k], 1), :], sem)

    _run_row_copies(copy, rows_ref.shape[0])


def _dispatch(rows, pos3, n_slots):
    T = rows.shape[0]
    tm = pos3.shape[2]
    init = jnp.zeros((n_slots, ROW_W), f32)
    return pl.pallas_call(
        _dispatch_kernel,
        grid=(T // tm,),
        in_specs=[pl.BlockSpec((1, 1, tm), lambda i: (i, 0, 0), memory_space=pltpu.SMEM),
                  pl.BlockSpec((tm, ROW_W), lambda i: (i, 0)),
                  pl.BlockSpec(memory_space=pl.ANY)],
        out_specs=pl.BlockSpec(memory_space=pl.ANY),
        out_shape=jax.ShapeDtypeStruct((n_slots, ROW_W), f32),
        scratch_shapes=[pltpu.SemaphoreType.DMA(())],
        input_output_aliases={2: 0},
        compiler_params=_params(1),
        name="moe_dispatch",
    )(pos3, rows, init)


def _swiglu(xt, w1, w3, w2):
    h1 = jnp.dot(xt, w1, preferred_element_type=f32)
    h3 = jnp.dot(xt, w3, preferred_element_type=f32)
    h = h1 * (1.0 / (1.0 + jnp.exp(-h1))) * h3
    return jnp.dot(h.astype(bf16), w2, preferred_element_type=f32)


def _moe_kernel(ea_ref, eb_ref, valid_ref, xs_ref, w1a, w3a, w2a, w1b, w3b, w2b, ys_ref):
    i = pl.program_id(0)

    @pl.when(valid_ref[i] != 0)
    def _():
        xt = xs_ref[:, :D_MODEL].astype(bf16)
        ga = xs_ref[:, D_MODEL:D_MODEL + 1]
        gb = xs_ref[:, D_MODEL + 1:D_MODEL + 2]
        ys_ref[...] = (ga * _swiglu(xt, w1a[...], w3a[...], w2a[...])
                       + gb * _swiglu(xt, w1b[...], w3b[...], w2b[...]))

    @pl.when(valid_ref[i] == 0)
    def _():
        ys_ref[...] = jnp.zeros_like(ys_ref)


def _moe(xs, tile_a, tile_b, tile_valid, w1, w3, w2, layer):
    n_slots = xs.shape[0]
    tm = MOE_TM
    wa = lambda shape: pl.BlockSpec((None, None) + shape, lambda i, ea, eb, va: (layer, ea[i], 0, 0))
    wb = lambda shape: pl.BlockSpec((None, None) + shape, lambda i, ea, eb, va: (layer, eb[i], 0, 0))
    up, down = (D_MODEL, D_EXPERT), (D_EXPERT, D_MODEL)
    grid_spec = pltpu.PrefetchScalarGridSpec(
        num_scalar_prefetch=3,
        grid=(n_slots // tm,),
        in_specs=[pl.BlockSpec((tm, ROW_W), lambda i, ea, eb, va: (i, 0)),
                  wa(up), wa(up), wa(down), wb(up), wb(up), wb(down)],
        out_specs=pl.BlockSpec((tm, D_MODEL), lambda i, ea, eb, va: (i, 0)),
    )
    return pl.pallas_call(
        _moe_kernel,
        grid_spec=grid_spec,
        out_shape=jax.ShapeDtypeStruct((n_slots, D_MODEL), f32),
        compiler_params=_params(1),
        name="moe_experts",
    )(tile_a, tile_b, tile_valid, xs, w1, w3, w2, w1, w3, w2)


def _combine_kernel(pos_ref, x_ref, ys_ref, g_ref, b_ref, o_ref, buf, sem):
    def copy(k):
        return pltpu.make_async_copy(ys_ref.at[pl.ds(pos_ref[0, 0, k], 1), :], buf.at[pl.ds(k, 1), :], sem)

    _run_row_copies(copy, x_ref.shape[0])
    o_ref[...] = _layer_norm(ALPHA * x_ref[...] + buf[...], g_ref[...], b_ref[...])


def _combine_ln(rows, ys, pos3, g, b):
    T = rows.shape[0]
    tm = pos3.shape[2]
    vec = pl.BlockSpec((1, D_MODEL), lambda i: (0, 0))
    return pl.pallas_call(
        _combine_kernel,
        grid=(T // tm,),
        in_specs=[pl.BlockSpec((1, 1, tm), lambda i: (i, 0, 0), memory_space=pltpu.SMEM),
                  pl.BlockSpec((tm, D_MODEL), lambda i: (i, 0)),
                  pl.BlockSpec(memory_space=pl.ANY), vec, vec],
        out_specs=pl.BlockSpec((tm, D_MODEL), lambda i: (i, 0)),
        out_shape=jax.ShapeDtypeStruct((T, D_MODEL), f32),
        scratch_shapes=[pltpu.VMEM((tm, D_MODEL), f32), pltpu.SemaphoreType.DMA(())],
        compiler_params=_params(1),
        name="moe_combine_ln",
    )(pos3, rows, ys, g, b)


def _plan(cls, n_tiles):
    onehot = (cls[:, None] == jnp.arange(N_CLASSES, dtype=jnp.int32)[None, :]).astype(jnp.int32)
    csum = jnp.cumsum(onehot, axis=0)
    rank = jnp.sum(csum * onehot, axis=1) - 1
    counts = csum[-1]
    tiles = (counts + MOE_TM - 1) // MOE_TM
    tile_end = jnp.cumsum(tiles)
    tile_start = tile_end - tiles
    pos = jnp.sum(onehot * tile_start[None, :], axis=1) * MOE_TM + rank
    tile_id = jnp.arange(n_tiles, dtype=jnp.int32)
    used = tile_end[-1]
    tcls = jnp.searchsorted(tile_end, jnp.minimum(tile_id, used - 1), side="right").astype(jnp.int32)
    grp = tcls // PAIRS_PER_GROUP
    pair = tcls % PAIRS_PER_GROUP
    pa, pb = np.triu_indices(EXPERTS_PER_GROUP, 1)
    ea = grp * EXPERTS_PER_GROUP + jnp.asarray(pa, jnp.int32)[pair]
    eb = grp * EXPERTS_PER_GROUP + jnp.asarray(pb, jnp.int32)[pair]
    valid = (tile_id < used).astype(jnp.int32)
    return pos.astype(jnp.int32), ea, eb, valid


def _t5_buckets(rel):
    nb = N_BUCKETS // 2
    max_exact = nb // 2
    ret = np.where(rel > 0, nb, 0)
    n = np.abs(rel)
    nf = np.maximum(n, 1).astype(np.float32)
    large = max_exact + (np.log(nf / np.float32(max_exact)) / np.float32(math.log(T5_MAX_DISTANCE / max_exact))
                         * np.float32(nb - max_exact)).astype(np.int32)
    large = np.minimum(large, nb - 1)
    return ret + np.where(n < max_exact, n, large)


def _table_lookup(table, idx):
    onehot = (jnp.asarray(idx, jnp.int32)[..., None] == jnp.arange(table.shape[0], dtype=jnp.int32)).astype(f32)
    return jnp.einsum("...n,nh->...h", onehot, table, precision=lax.Precision.HIGHEST)


def _dilated_bias(t5_table, dil):
    qi = np.arange(QBLK)[:, None]
    kj = np.arange(KBLK)[None, :]
    out = []
    for shift in (0, HALF, 2 * HALF):
        rel = kj - qi - shift
        band = np.abs(rel) <= HALF
        vals = _table_lookup(t5_table.astype(f32), _t5_buckets(rel * dil))
        vals = jnp.where(jnp.asarray(band)[:, :, None], vals, NEG_INF)
        out.append(vals.transpose(2, 0, 1))
    tab = jnp.stack(out, axis=1)
    tab = tab.reshape(HEAD_PAIRS, 2, 3, QBLK, KBLK).transpose(0, 2, 1, 3, 4)
    return tab.reshape(HEAD_PAIRS, 3, 2 * QBLK, KBLK)


def _neighbourhood_bias(rpb, rows):
    kh = min(NA_KH, rows)
    nvar = NA_KH if rows > kh else 1
    j = np.arange(GRID_W)[:, None]
    kc = np.arange(GRID_W)[None, :]
    ws = np.clip(j - NA_KW // 2, 0, GRID_W - NA_KW)
    inwin = (kc >= ws) & (kc < ws + NA_KW)
    dc = np.clip(kc - j + NA_KW - 1, 0, 2 * NA_KW - 2)
    ndr, ndc = rpb.shape[1], rpb.shape[2]
    cols = _table_lookup(rpb.astype(f32).reshape(N_HEADS * ndr, ndc).T, dc)
    cols = jnp.where(jnp.asarray(inwin)[:, :, None], cols, NEG_INF)
    cols = cols.reshape(GRID_W, GRID_W, N_HEADS, ndr).transpose(2, 3, 0, 1)
    out = []
    for var in range(nvar):
        lo = var + (NA_KH - kh)
        vals = cols[:, lo:lo + kh]
        out.append(vals.transpose(0, 2, 1, 3).reshape(N_HEADS, GRID_W, kh * GRID_W))
    tab = jnp.stack(out, axis=1)
    tab = tab.reshape(HEAD_PAIRS, 2, nvar, GRID_W, kh * GRID_W).transpose(0, 2, 1, 3, 4)
    return tab.reshape(HEAD_PAIRS, nvar, 2 * GRID_W, kh * GRID_W)


def _qkv_weight(w):
    hd = N_HEADS * HEAD_DIM
    scale = jnp.concatenate([jnp.full((hd,), HEAD_DIM ** -0.5, f32), jnp.ones((2 * hd,), f32)])
    return (w * scale[None, :]).astype(bf16)


def _router_weight(w_rg, b_rg, w_re, b_re):
    pad = LANES - N_GROUPS - N_EXPERTS
    wr = jnp.concatenate([w_rg, w_re, jnp.zeros((D_MODEL, pad), f32)], axis=1)
    br = jnp.concatenate([b_rg, b_re, jnp.zeros((pad,), f32)])[None, :]
    hi = wr.astype(bf16)
    lo = (wr - hi.astype(f32)).astype(bf16)
    return jnp.concatenate([hi, lo], axis=1), br


def _trunk(x3, t5_table, w_qkv_a, w_o_a, w_qkv_b, w_o_b, rpb_b, ln_g, ln_b,
           w_rg, b_rg, w_re, b_re, w1, w3, w2):
    B, S, D = x3.shape
    T = B * S
    n_slots = T + N_CLASSES * MOE_TM
    n_tiles = n_slots // MOE_TM
    hd = N_HEADS * HEAD_DIM
    x = x3.reshape(T, D)
    w1b, w3b, w2b = w1.astype(bf16), w3.astype(bf16), w2.astype(bf16)
    bias_a = [_dilated_bias(t5_table, dil) for _, dil in DILATIONS]
    for i in range(DEPTH):
        j = i // 2
        x3 = x.reshape(B, S, D)
        if i % 2 == 0:
            qkvs = [_qkv_proj(x3, _qkv_weight(w_qkv_a[j][:, g * 3 * hd:(g + 1) * 3 * hd]), dil)
                    for g, (_, dil) in enumerate(DILATIONS)]
            att = _attn_a(qkvs, bias_a)
            wo = w_o_a[j]
        else:
            qkv = _qkv_proj(x3, _qkv_weight(w_qkv_b[j]), 1)
            att = _attn_b(qkv, _neighbourhood_bias(rpb_b[j], S // GRID_W))
            wo = w_o_b[j]
        wr, br = _router_weight(w_rg[i], b_rg[i], w_re[i], b_re[i])
        rows = _oproj_ln_route(att, x, wo.astype(bf16), ln_g[i, 0][None, :], ln_b[i, 0][None, :], wr, br)
        cls = rows[:, D_MODEL + 2].astype(jnp.int32)
        pos, ea, eb, valid = _plan(cls, n_tiles)
        pos3 = pos.reshape(T // TOK_TM, 1, TOK_TM)
        xs = _dispatch(rows, pos3, n_slots)
        ys = _moe(xs, ea, eb, valid, w1b, w3b, w2b, i)
        x = _combine_ln(rows, ys, pos3, ln_g[i, 1][None, :], ln_b[i, 1][None, :])
    return x.reshape(B, S, D)


def kernel(x_prompt, x_sample, t5_table, w_qkv_a, w_o_a, w_qkv_b, w_o_b, rpb_b, ln_g, ln_b,
           w_rg, b_rg, w_re, b_re, w1, w3, w2):
    nb = x_prompt.shape[0]
    x = jnp.concatenate([x_prompt, x_sample], axis=0)
    y = _trunk(x, t5_table, w_qkv_a, w_o_a, w_qkv_b, w_o_b, rpb_b, ln_g, ln_b,
               w_rg, b_rg, w_re, b_re, w1, w3, w2)
    return (y[:nb], y[nb:])
```

```python
import functools
import math

import numpy as np
import jax
import jax.numpy as jnp
from jax import lax
from jax.experimental import pallas as pl
from jax.experimental.pallas import tpu as pltpu

D_MODEL = 1024
SEQ = 4096
DEPTH = 4
HEAD_DIM = 64
N_HEADS = 16
DILATIONS = ((128, 1), (512, 4), (2048, 16))
N_BUCKETS = 32
T5_MAX_DISTANCE = 1024
GRID_W = 64
NA_KH = 8
NA_KW = 16
N_GROUPS = 4
EXPERTS_PER_GROUP = 8
N_EXPERTS = N_GROUPS * EXPERTS_PER_GROUP
D_EXPERT = 512
ALPHA = (2 * DEPTH) ** 0.25
LN_EPS = 1e-5
NEG_INF = -1e30

LANES = 128
SUBLANES = 8
HEAD_PAIRS = N_HEADS * HEAD_DIM // LANES
QKV_BLOCKS = 3 * HEAD_PAIRS
VMEM_LIMIT = 56 * 1024 * 1024

QBLK = 128
KBLK = 256
HALF = 64
NA_ROWS_PER_ITER = 8
DIL_BLOCKS_PER_ITER = 4
PAIRS_PER_GROUP = EXPERTS_PER_GROUP * (EXPERTS_PER_GROUP - 1) // 2
N_CLASSES = N_GROUPS * PAIRS_PER_GROUP
ROW_W = D_MODEL + LANES
MOE_TM = 256
TOK_TM = 512
DMA_UNROLL = 8

f32 = jnp.float32
bf16 = jnp.bfloat16


def _params(n_grid):
    return pltpu.CompilerParams(dimension_semantics=("arbitrary",) * n_grid,
                                vmem_limit_bytes=VMEM_LIMIT)


def _qkv_kernel(x_ref, w_ref, o_ref, acc_ref, *, dil, tm, cw):
    nsub = cw // LANES
    rows = tm // dil
    rows_in = len(x_ref.shape) == 3
    if rows_in:
        xb = jnp.concatenate([x_ref[:, r, :] for r in range(dil)], axis=0).astype(bf16)
    else:
        xb = x_ref[...].astype(bf16)
    for j in range(QKV_BLOCKS // nsub):
        res = jnp.dot(xb, w_ref[:, j * cw:(j + 1) * cw], preferred_element_type=f32)
        if dil == 1 or rows_in:
            for kk in range(nsub):
                for r in range(dil):
                    o_ref[j * nsub + kk, r, :, :] = res[r * rows:(r + 1) * rows,
                                                        kk * LANES:(kk + 1) * LANES].astype(bf16)
            continue
        for kk in range(nsub):
            acc_ref[kk] = res[:, kk * LANES:(kk + 1) * LANES]
        for kk in range(nsub):
            for r in range(dil):
                o_ref[j * nsub + kk, r, :, :] = acc_ref[kk, pl.ds(r, rows, stride=dil), :].astype(bf16)


def _qkv_proj(x3, w, dil):
    B, S, D = x3.shape
    tm, cw = 1024, 512
    L = S // dil
    if dil % SUBLANES == 0:
        x3 = x3.reshape(B, L, dil, D)
        x_spec = pl.BlockSpec((None, tm // dil, dil, D), lambda b, i: (b, i, 0, 0))
    else:
        x_spec = pl.BlockSpec((None, tm, D), lambda b, i: (b, i, 0))
    return pl.pallas_call(
        functools.partial(_qkv_kernel, dil=dil, tm=tm, cw=cw),
        grid=(B, S // tm),
        in_specs=[x_spec,
                  pl.BlockSpec((D, QKV_BLOCKS * LANES), lambda b, i: (0, 0))],
        out_specs=pl.BlockSpec((None, QKV_BLOCKS, dil, tm // dil, LANES), lambda b, i: (b, 0, 0, i, 0)),
        out_shape=jax.ShapeDtypeStruct((B, QKV_BLOCKS, dil, L, LANES), bf16),
        scratch_shapes=[pltpu.VMEM((cw // LANES, tm, LANES), f32)],
        compiler_params=_params(2),
        name=f"qkv_proj_d{dil}",
    )(x3, w)


def _softmax_block(qst, kb, vb, bias):
    nq = qst.shape[0] // 2
    s = lax.dot_general(qst, kb, (((1,), (1,)), ((), ())), preferred_element_type=f32) + bias
    m = jnp.max(s, axis=1, keepdims=True)
    p = jnp.exp((s - m).astype(bf16))
    v1 = jnp.concatenate([vb, jnp.ones_like(vb)], axis=1)
    ol = jnp.dot(p, v1, preferred_element_type=f32)
    l = ol[:, LANES:]
    o2 = ol[:, :LANES] * (1.0 / l)
    lse = m + jnp.log(l)
    lane = lax.broadcasted_iota(jnp.int32, (nq, LANES), 1)
    o = jnp.where(lane < HEAD_DIM, o2[:nq], o2[nq:])
    ls = jnp.where(lane < HEAD_DIM, lse[:nq], lse[nq:])
    return o, ls


def _stack_heads(q2):
    lane = lax.broadcasted_iota(jnp.int32, q2.shape, 1)
    zero = jnp.zeros_like(q2)
    return jnp.concatenate([jnp.where(lane < HEAD_DIM, q2, zero),
                            jnp.where(lane >= HEAD_DIM, q2, zero)], axis=0)


def _attn_a_kernel(*refs):
    qkv = refs[:9]
    biases = refs[9:12]
    o_ref = refs[12]
    oscr, lscr = refs[13], refs[14]
    S = o_ref.shape[0]

    def block(it, g):
        dil = DILATIONS[g][1]
        q_ref, k_ref, v_ref = qkv[3 * g:3 * g + 3]
        L = S // dil
        nb = L // QBLK
        r = it // nb
        n = it % nb
        qs = pl.multiple_of(n * QBLK, QBLK)
        ks = pl.multiple_of(jnp.clip(qs - HALF, 0, L - KBLK), HALF)
        var = jnp.where(n == 0, 0, jnp.where(n == nb - 1, 2, 1))
        qst = _stack_heads(q_ref[r, pl.ds(qs, QBLK), :])
        o, ls = _softmax_block(qst, k_ref[r, pl.ds(ks, KBLK), :], v_ref[r, pl.ds(ks, KBLK), :], biases[g][var])
        if dil == 1:
            dst = pl.ds(qs, QBLK)
        else:
            dst = pl.ds(qs * dil + r, QBLK, stride=dil)
        oscr[g, dst, :] = o
        lscr[g, dst, :] = ls

    def body(it, carry):
        for u in range(DIL_BLOCKS_PER_ITER):
            for g in range(len(DILATIONS)):
                block(it * DIL_BLOCKS_PER_ITER + u, g)
        return carry

    lax.fori_loop(0, S // QBLK // DIL_BLOCKS_PER_ITER, body, 0)

    ch = 256

    def merge(c, carry):
        rows = pl.ds(pl.multiple_of(c * ch, ch), ch)
        l0, l1, l2 = lscr[0, rows, :], lscr[1, rows, :], lscr[2, rows, :]
        mx = jnp.maximum(jnp.maximum(l0, l1), l2)
        e0, e1, e2 = jnp.exp(l0 - mx), jnp.exp(l1 - mx), jnp.exp(l2 - mx)
        inv = 1.0 / (e0 + e1 + e2)
        o = (e0 * oscr[0, rows, :] + e1 * oscr[1, rows, :] + e2 * oscr[2, rows, :]) * inv
        o_ref[rows, :] = o.astype(bf16)
        return carry

    lax.fori_loop(0, S // ch, merge, 0)


def _attn_a(qkvs, biases):
    B = qkvs[0].shape[0]
    S = qkvs[0].shape[2] * qkvs[0].shape[3]
    in_specs, args = [], []
    for g, (_, dil) in enumerate(DILATIONS):
        L = S // dil
        for c in range(3):
            in_specs.append(pl.BlockSpec((None, None, dil, L, LANES),
                                         lambda b, hp, c=c: (b, c * HEAD_PAIRS + hp, 0, 0, 0)))
            args.append(qkvs[g])
    for g in range(3):
        in_specs.append(pl.BlockSpec((None, 3, 2 * QBLK, KBLK), lambda b, hp: (hp, 0, 0, 0)))
        args.append(biases[g])
    return pl.pallas_call(
        _attn_a_kernel,
        grid=(B, HEAD_PAIRS),
        in_specs=in_specs,
        out_specs=pl.BlockSpec((None, None, S, LANES), lambda b, hp: (b, hp, 0, 0)),
        out_shape=jax.ShapeDtypeStruct((B, HEAD_PAIRS, S, LANES), bf16),
        scratch_shapes=[pltpu.VMEM((3, S, LANES), f32), pltpu.VMEM((3, S, LANES), f32)],
        compiler_params=_params(2),
        name="dilated_attn",
    )(*args)


def _attn_b_kernel(q_ref, k_ref, v_ref, b_ref, o_ref):
    rows = o_ref.shape[0] // GRID_W
    kh = min(NA_KH, rows)
    nk = kh * GRID_W

    def one_row(i):
        rs = jnp.clip(i - kh // 2, 0, rows - kh)
        var = rs - i + NA_KH - 1 - (NA_KH - kh)
        qs = pl.multiple_of(i * GRID_W, GRID_W)
        ks = pl.multiple_of(rs * GRID_W, GRID_W)
        qst = _stack_heads(q_ref[pl.ds(qs, GRID_W), :])
        o, _ = _softmax_block(qst, k_ref[pl.ds(ks, nk), :], v_ref[pl.ds(ks, nk), :], b_ref[var])
        o_ref[pl.ds(qs, GRID_W), :] = o.astype(bf16)

    def body(it, carry):
        for u in range(NA_ROWS_PER_ITER):
            one_row(it * NA_ROWS_PER_ITER + u)
        return carry

    lax.fori_loop(0, rows // NA_ROWS_PER_ITER, body, 0)


def _attn_b(qkv, bias):
    B, _, _, S, _ = qkv.shape
    nvar, nk = bias.shape[1], bias.shape[3]
    in_specs = [pl.BlockSpec((None, None, None, S, LANES), lambda b, hp, c=c: (b, c * HEAD_PAIRS + hp, 0, 0, 0))
                for c in range(3)]
    in_specs.append(pl.BlockSpec((None, nvar, 2 * GRID_W, nk), lambda b, hp: (hp, 0, 0, 0)))
    return pl.pallas_call(
        _attn_b_kernel,
        grid=(B, HEAD_PAIRS),
        in_specs=in_specs,
        out_specs=pl.BlockSpec((None, None, S, LANES), lambda b, hp: (b, hp, 0, 0)),
        out_shape=jax.ShapeDtypeStruct((B, HEAD_PAIRS, S, LANES), bf16),
        compiler_params=_params(2),
        name="neighbourhood_attn",
    )(qkv, qkv, qkv, bias)


def _layer_norm(y, g, b):
    mu = jnp.mean(y, axis=1, keepdims=True)
    yc = y - mu
    var = jnp.mean(yc * yc, axis=1, keepdims=True)
    return yc * lax.rsqrt(var + LN_EPS) * g + b


def _route(logits):
    shape = logits.shape
    lane = lax.broadcasted_iota(jnp.int32, shape, 1)
    big = jnp.int32(4 * LANES)
    is_g = lane < N_GROUPS
    gl = jnp.where(is_g, logits, NEG_INF)
    mg = jnp.max(gl, axis=1, keepdims=True)
    gsel = jnp.min(jnp.where(gl == mg, lane, big), axis=1, keepdims=True)
    p_sel = 1.0 / jnp.sum(jnp.where(is_g, jnp.exp(gl - mg), 0.0), axis=1, keepdims=True)
    lo = N_GROUPS + gsel * EXPERTS_PER_GROUP
    el = jnp.where((lane >= lo) & (lane < lo + EXPERTS_PER_GROUP), logits, NEG_INF)
    v1 = jnp.max(el, axis=1, keepdims=True)
    i1 = jnp.min(jnp.where(el == v1, lane, big), axis=1, keepdims=True)
    el2 = jnp.where(lane == i1, NEG_INF, el)
    v2 = jnp.max(el2, axis=1, keepdims=True)
    i2 = jnp.min(jnp.where(el2 == v2, lane, big), axis=1, keepdims=True)
    t = jnp.exp(v2 - v1)
    g1 = p_sel / (1.0 + t)
    g2 = p_sel * t / (1.0 + t)
    first_lo = i1 < i2
    a = jnp.where(first_lo, i1, i2) - lo
    b = jnp.where(first_lo, i2, i1) - lo
    ga = jnp.where(first_lo, g1, g2)
    gb = jnp.where(first_lo, g2, g1)
    pair = (EXPERTS_PER_GROUP - 1) * a - ((a * (a - 1)) >> 1) + (b - a - 1)
    cls = (gsel * PAIRS_PER_GROUP + pair).astype(f32)
    return jnp.where(lane == 0, ga, jnp.where(lane == 1, gb, jnp.where(lane == 2, cls, 0.0)))


def _oproj_kernel(att_ref, x_ref, wo_ref, g_ref, b_ref, wr_ref, br_ref, o_ref):
    a = jnp.concatenate([att_ref[h] for h in range(HEAD_PAIRS)], axis=1)
    h = jnp.dot(a, wo_ref[...], preferred_element_type=f32)
    xn = _layer_norm(ALPHA * x_ref[...] + h, g_ref[...], b_ref[...])
    xh = xn.astype(bf16)
    xl = (xn - xh.astype(f32)).astype(bf16)
    hw = jnp.dot(xh, wr_ref[...], preferred_element_type=f32)
    lw = jnp.dot(xl, wr_ref[:, :LANES], preferred_element_type=f32)
    logits = hw[:, :LANES] + hw[:, LANES:] + lw + br_ref[...]
    o_ref[:, :D_MODEL] = xn
    o_ref[:, D_MODEL:] = _route(logits)


def _oproj_ln_route(att, x, wo, g, b, wr, br):
    B, _, S, _ = att.shape
    T = B * S
    tm = TOK_TM
    per = S // tm
    vec = lambda n: pl.BlockSpec((1, n), lambda i: (0, 0))
    return pl.pallas_call(
        _oproj_kernel,
        grid=(T // tm,),
        in_specs=[pl.BlockSpec((None, HEAD_PAIRS, tm, LANES), lambda i: (i // per, 0, i % per, 0)),
                  pl.BlockSpec((tm, D_MODEL), lambda i: (i, 0)),
                  pl.BlockSpec((D_MODEL, D_MODEL), lambda i: (0, 0)),
                  vec(D_MODEL), vec(D_MODEL),
                  pl.BlockSpec((D_MODEL, 2 * LANES), lambda i: (0, 0)), vec(LANES)],
        out_specs=pl.BlockSpec((tm, ROW_W), lambda i: (i, 0)),
        out_shape=jax.ShapeDtypeStruct((T, ROW_W), f32),
        compiler_params=_params(1),
        name="oproj_ln_route",
    )(att, x, wo, g, b, wr, br)


def _run_row_copies(copy, n):
    def start(c, carry):
        for u in range(DMA_UNROLL):
            copy(c * DMA_UNROLL + u).start()
        return carry

    def wait(c, carry):
        for u in range(DMA_UNROLL):
            copy(c * DMA_UNROLL + u).wait()
        return carry

    lax.fori_loop(0, n // DMA_UNROLL, start, 0)
    lax.fori_loop(0, n // DMA_UNROLL, wait, 0)


def _dispatch_kernel(pos_ref, rows_ref, init_ref, xs_ref, sem):
    del init_ref

    def copy(k):
        return pltpu.make_async_copy(rows_ref.at[pl.ds(k, 1), :], xs_ref.at[pl.ds(pos_ref[0, 0, k], 1), :], sem)

    _run_row_copies(copy, rows_ref.shape[0])


def _dispatch(rows, pos3, n_slots):
    T = rows.shape[0]
    tm = pos3.shape[2]
    init = jnp.zeros((n_slots, ROW_W), f32)
    return pl.pallas_call(
        _dispatch_kernel,
        grid=(T // tm,),
        in_specs=[pl.BlockSpec((1, 1, tm), lambda i: (i, 0, 0), memory_space=pltpu.SMEM),
                  pl.BlockSpec((tm, ROW_W), lambda i: (i, 0)),
                  pl.BlockSpec(memory_space=pl.ANY)],
        out_specs=pl.BlockSpec(memory_space=pl.ANY),
        out_shape=jax.ShapeDtypeStruct((n_slots, ROW_W), f32),
        scratch_shapes=[pltpu.SemaphoreType.DMA(())],
        input_output_aliases={2: 0},
        compiler_params=_params(1),
        name="moe_dispatch",
    )(pos3, rows, init)


def _swiglu(xt, w1, w3, w2):
    h1 = jnp.dot(xt, w1, preferred_element_type=f32)
    h3 = jnp.dot(xt, w3, preferred_element_type=f32)
    h = h1 * (1.0 / (1.0 + jnp.exp(-h1))) * h3
    return jnp.dot(h.astype(bf16), w2, preferred_element_type=f32)


def _moe_kernel(ea_ref, eb_ref, valid_ref, xs_ref, w1a, w3a, w2a, w1b, w3b, w2b, ys_ref):
    i = pl.program_id(0)

    @pl.when(valid_ref[i] != 0)
    def _():
        xt = xs_ref[:, :D_MODEL].astype(bf16)
        ga = xs_ref[:, D_MODEL:D_MODEL + 1]
        gb = xs_ref[:, D_MODEL + 1:D_MODEL + 2]
        ys_ref[...] = (ga * _swiglu(xt, w1a[...], w3a[...], w2a[...])
                       + gb * _swiglu(xt, w1b[...], w3b[...], w2b[...]))

    @pl.when(valid_ref[i] == 0)
    def _():
        ys_ref[...] = jnp.zeros_like(ys_ref)


def _moe(xs, tile_a, tile_b, tile_valid, w1, w3, w2, layer):
    n_slots = xs.shape[0]
    tm = MOE_TM
    wa = lambda shape: pl.BlockSpec((None, None) + shape, lambda i, ea, eb, va: (layer, ea[i], 0, 0))
    wb = lambda shape: pl.BlockSpec((None, None) + shape, lambda i, ea, eb, va: (layer, eb[i], 0, 0))
    up, down = (D_MODEL, D_EXPERT), (D_EXPERT, D_MODEL)
    grid_spec = pltpu.PrefetchScalarGridSpec(
        num_scalar_prefetch=3,
        grid=(n_slots // tm,),
        in_specs=[pl.BlockSpec((tm, ROW_W), lambda i, ea, eb, va: (i, 0)),
                  wa(up), wa(up), wa(down), wb(up), wb(up), wb(down)],
        out_specs=pl.BlockSpec((tm, D_MODEL), lambda i, ea, eb, va: (i, 0)),
    )
    return pl.pallas_call(
        _moe_kernel,
        grid_spec=grid_spec,
        out_shape=jax.ShapeDtypeStruct((n_slots, D_MODEL), f32),
        compiler_params=_params(1),
        name="moe_experts",
    )(tile_a, tile_b, tile_valid, xs, w1, w3, w2, w1, w3, w2)


def _combine_kernel(pos_ref, x_ref, ys_ref, g_ref, b_ref, o_ref, buf, sem):
    def copy(k):
        return pltpu.make_async_copy(ys_ref.at[pl.ds(pos_ref[0, 0, k], 1), :], buf.at[pl.ds(k, 1), :], sem)

    _run_row_copies(copy, x_ref.shape[0])
    o_ref[...] = _layer_norm(ALPHA * x_ref[...] + buf[...], g_ref[...], b_ref[...])


def _combine_ln(rows, ys, pos3, g, b):
    T = rows.shape[0]
    tm = pos3.shape[2]
    vec = pl.BlockSpec((1, D_MODEL), lambda i: (0, 0))
    return pl.pallas_call(
        _combine_kernel,
        grid=(T // tm,),
        in_specs=[pl.BlockSpec((1, 1, tm), lambda i: (i, 0, 0), memory_space=pltpu.SMEM),
                  pl.BlockSpec((tm, D_MODEL), lambda i: (i, 0)),
                  pl.BlockSpec(memory_space=pl.ANY), vec, vec],
        out_specs=pl.BlockSpec((tm, D_MODEL), lambda i: (i, 0)),
        out_shape=jax.ShapeDtypeStruct((T, D_MODEL), f32),
        scratch_shapes=[pltpu.VMEM((tm, D_MODEL), f32), pltpu.SemaphoreType.DMA(())],
        compiler_params=_params(1),
        name="moe_combine_ln",
    )(pos3, rows, ys, g, b)


def _plan(cls, n_tiles):
    onehot = (cls[:, None] == jnp.arange(N_CLASSES, dtype=jnp.int32)[None, :]).astype(jnp.int32)
    csum = jnp.cumsum(onehot, axis=0)
    rank = jnp.sum(csum * onehot, axis=1) - 1
    counts = csum[-1]
    tiles = (counts + MOE_TM - 1) // MOE_TM
    tile_end = jnp.cumsum(tiles)
    tile_start = tile_end - tiles
    pos = jnp.sum(onehot * tile_start[None, :], axis=1) * MOE_TM + rank
    tile_id = jnp.arange(n_tiles, dtype=jnp.int32)
    used = tile_end[-1]
    tcls = jnp.searchsorted(tile_end, jnp.minimum(tile_id, used - 1), side="right").astype(jnp.int32)
    grp = tcls // PAIRS_PER_GROUP
    pair = tcls % PAIRS_PER_GROUP
    pa, pb = np.triu_indices(EXPERTS_PER_GROUP, 1)
    ea = grp * EXPERTS_PER_GROUP + jnp.asarray(pa, jnp.int32)[pair]
    eb = grp * EXPERTS_PER_GROUP + jnp.asarray(pb, jnp.int32)[pair]
    valid = (tile_id < used).astype(jnp.int32)
    return pos.astype(jnp.int32), ea, eb, valid


def _t5_buckets(rel):
    nb = N_BUCKETS // 2
    max_exact = nb // 2
    ret = np.where(rel > 0, nb, 0)
    n = np.abs(rel)
    nf = np.maximum(n, 1).astype(np.float32)
    large = max_exact + (np.log(nf / np.float32(max_exact)) / np.float32(math.log(T5_MAX_DISTANCE / max_exact))
                         * np.float32(nb - max_exact)).astype(np.int32)
    large = np.minimum(large, nb - 1)
    return ret + np.where(n < max_exact, n, large)


def _table_lookup(table, idx):
    onehot = (jnp.asarray(idx, jnp.int32)[..., None] == jnp.arange(table.shape[0], dtype=jnp.int32)).astype(f32)
    return jnp.einsum("...n,nh->...h", onehot, table, precision=lax.Precision.HIGHEST)


def _dilated_bias(t5_table, dil):
    qi = np.arange(QBLK)[:, None]
    kj = np.arange(KBLK)[None, :]
    out = []
    for shift in (0, HALF, 2 * HALF):
        rel = kj - qi - shift
        band = np.abs(rel) <= HALF
        vals = _table_lookup(t5_table.astype(f32), _t5_buckets(rel * dil))
        vals = jnp.where(jnp.asarray(band)[:, :, None], vals, NEG_INF)
        out.append(vals.transpose(2, 0, 1))
    tab = jnp.stack(out, axis=1)
    tab = tab.reshape(HEAD_PAIRS, 2, 3, QBLK, KBLK).transpose(0, 2, 1, 3, 4)
    return tab.reshape(HEAD_PAIRS, 3, 2 * QBLK, KBLK)


def _neighbourhood_bias(rpb, rows):
    kh = min(NA_KH, rows)
    nvar = NA_KH if rows > kh else 1
    j = np.arange(GRID_W)[:, None]
    kc = np.arange(GRID_W)[None, :]
    ws = np.clip(j - NA_KW // 2, 0, GRID_W - NA_KW)
    inwin = (kc >= ws) & (kc < ws + NA_KW)
    dc = np.clip(kc - j + NA_KW - 1, 0, 2 * NA_KW - 2)
    ndr, ndc = rpb.shape[1], rpb.shape[2]
    cols = _table_lookup(rpb.astype(f32).reshape(N_HEADS * ndr, ndc).T, dc)
    cols = jnp.where(jnp.asarray(inwin)[:, :, None], cols, NEG_INF)
    cols = cols.reshape(GRID_W, GRID_W, N_HEADS, ndr).transpose(2, 3, 0, 1)
    out = []
    for var in range(nvar):
        lo = var + (NA_KH - kh)
        vals = cols[:, lo:lo + kh]
        out.append(vals.transpose(0, 2, 1, 3).reshape(N_HEADS, GRID_W, kh * GRID_W))
    tab = jnp.stack(out, axis=1)
    tab = tab.reshape(HEAD_PAIRS, 2, nvar, GRID_W, kh * GRID_W).transpose(0, 2, 1, 3, 4)
    return tab.reshape(HEAD_PAIRS, nvar, 2 * GRID_W, kh * GRID_W)


def _qkv_weight(w):
    hd = N_HEADS * HEAD_DIM
    scale = jnp.concatenate([jnp.full((hd,), HEAD_DIM ** -0.5, f32), jnp.ones((2 * hd,), f32)])
    return (w * scale[None, :]).astype(bf16)


def _router_weight(w_rg, b_rg, w_re, b_re):
    pad = LANES - N_GROUPS - N_EXPERTS
    wr = jnp.concatenate([w_rg, w_re, jnp.zeros((D_MODEL, pad), f32)], axis=1)
    br = jnp.concatenate([b_rg, b_re, jnp.zeros((pad,), f32)])[None, :]
    hi = wr.astype(bf16)
    lo = (wr - hi.astype(f32)).astype(bf16)
    return jnp.concatenate([hi, lo], axis=1), br


def _trunk(x3, t5_table, w_qkv_a, w_o_a, w_qkv_b, w_o_b, rpb_b, ln_g, ln_b,
           w_rg, b_rg, w_re, b_re, w1, w3, w2):
    B, S, D = x3.shape
    T = B * S
    n_slots = T + N_CLASSES * MOE_TM
    n_tiles = n_slots // MOE_TM
    hd = N_HEADS * HEAD_DIM
    x = x3.reshape(T, D)
    w1b, w3b, w2b = w1.astype(bf16), w3.astype(bf16), w2.astype(bf16)
    bias_a = [_dilated_bias(t5_table, dil) for _, dil in DILATIONS]
    for i in range(DEPTH):
        j = i // 2
        x3 = x.reshape(B, S, D)
        if i % 2 == 0:
            qkvs = [_qkv_proj(x3, _qkv_weight(w_qkv_a[j][:, g * 3 * hd:(g + 1) * 3 * hd]), dil)
                    for g, (_, dil) in enumerate(DILATIONS)]
            att = _attn_a(qkvs, bias_a)
            wo = w_o_a[j]
        else:
            qkv = _qkv_proj(x3, _qkv_weight(w_qkv_b[j]), 1)
            att = _attn_b(qkv, _neighbourhood_bias(rpb_b[j], S // GRID_W))
            wo = w_o_b[j]
        wr, br = _router_weight(w_rg[i], b_rg[i], w_re[i], b_re[i])
        rows = _oproj_ln_route(att, x, wo.astype(bf16), ln_g[i, 0][None, :], ln_b[i, 0][None, :], wr, br)
        cls = rows[:, D_MODEL + 2].astype(jnp.int32)
        pos, ea, eb, valid = _plan(cls, n_tiles)
        pos3 = pos.reshape(T // TOK_TM, 1, TOK_TM)
        xs = _dispatch(rows, pos3, n_slots)
        ys = _moe(xs, ea, eb, valid, w1b, w3b, w2b, i)
        x = _combine_ln(rows, ys, pos3, ln_g[i, 1][None, :], ln_b[i, 1][None, :])
    return x.reshape(B, S, D)


def kernel(x_prompt, x_sample, t5_table, w_qkv_a, w_o_a, w_qkv_b, w_o_b, rpb_b, ln_g, ln_b,
           w_rg, b_rg, w_re, b_re, w1, w3, w2):
    nb = x_prompt.shape[0]
    x = jnp.concatenate([x_prompt, x_sample], axis=0)
    y = _trunk(x, t5_table, w_qkv_a, w_o_a, w_qkv_b, w_o_b, rpb_b, ln_g, ln_b,
               w_rg, b_rg, w_re, b_re, w1, w3, w2)
    return (y[:nb], y[nb:])
```

```python
import functools
import math

import numpy as np
import jax
import jax.numpy as jnp
from jax import lax
from jax.experimental import pallas as pl
from jax.experimental.pallas import tpu as pltpu

D_MODEL = 1024
SEQ = 4096
DEPTH = 4
HEAD_DIM = 64
N_HEADS = 16
DILATIONS = ((128, 1), (512, 4), (2048, 16))
N_BUCKETS = 32
T5_MAX_DISTANCE = 1024
GRID_W = 64
NA_KH = 8
NA_KW = 16
N_GROUPS = 4
EXPERTS_PER_GROUP = 8
N_EXPERTS = N_GROUPS * EXPERTS_PER_GROUP
D_EXPERT = 512
ALPHA = (2 * DEPTH) ** 0.25
LN_EPS = 1e-5
NEG_INF = -1e30

LANES = 128
SUBLANES = 8
HEAD_PAIRS = N_HEADS * HEAD_DIM // LANES
QKV_BLOCKS = 3 * HEAD_PAIRS
VMEM_LIMIT = 56 * 1024 * 1024

QBLK = 128
KBLK = 256
HALF = 64
NA_ROWS_PER_ITER = 32
DIL_BLOCKS_PER_ITER = 16
PAIRS_PER_GROUP = EXPERTS_PER_GROUP * (EXPERTS_PER_GROUP - 1) // 2
N_CLASSES = N_GROUPS * PAIRS_PER_GROUP
ROW_W = D_MODEL + LANES
MOE_TM = 256
TOK_TM = 512
ROW_DMA_TM = 1024
DMA_UNROLL = 8

f32 = jnp.float32
bf16 = jnp.bfloat16


def _params(n_grid):
    return pltpu.CompilerParams(dimension_semantics=("arbitrary",) * n_grid,
                                vmem_limit_bytes=VMEM_LIMIT)


def _qkv_kernel(x_ref, w_ref, o_ref, acc_ref, *, dil, tm, cw):
    nsub = cw // LANES
    rows = tm // dil
    rows_in = len(x_ref.shape) == 3
    if rows_in:
        xb = jnp.concatenate([x_ref[:, r, :] for r in range(dil)], axis=0).astype(bf16)
    else:
        xb = x_ref[...].astype(bf16)
    for j in range(QKV_BLOCKS // nsub):
        res = jnp.dot(xb, w_ref[:, j * cw:(j + 1) * cw], preferred_element_type=f32)
        if dil == 1 or rows_in:
            for kk in range(nsub):
                for r in range(dil):
                    o_ref[j * nsub + kk, r, :, :] = res[r * rows:(r + 1) * rows,
                                                        kk * LANES:(kk + 1) * LANES].astype(bf16)
            continue
        for kk in range(nsub):
            acc_ref[kk] = res[:, kk * LANES:(kk + 1) * LANES]
        for kk in range(nsub):
            for r in range(dil):
                o_ref[j * nsub + kk, r, :, :] = acc_ref[kk, pl.ds(r, rows, stride=dil), :].astype(bf16)


def _qkv_proj(x3, w, dil):
    B, S, D = x3.shape
    tm, cw = 1024, 512
    L = S // dil
    if dil % SUBLANES == 0:
        x3 = x3.reshape(B, L, dil, D)
        x_spec = pl.BlockSpec((None, tm // dil, dil, D), lambda b, i: (b, i, 0, 0))
    else:
        x_spec = pl.BlockSpec((None, tm, D), lambda b, i: (b, i, 0))
    return pl.pallas_call(
        functools.partial(_qkv_kernel, dil=dil, tm=tm, cw=cw),
        grid=(B, S // tm),
        in_specs=[x_spec,
                  pl.BlockSpec((D, QKV_BLOCKS * LANES), lambda b, i: (0, 0))],
        out_specs=pl.BlockSpec((None, QKV_BLOCKS, dil, tm // dil, LANES), lambda b, i: (b, 0, 0, i, 0)),
        out_shape=jax.ShapeDtypeStruct((B, QKV_BLOCKS, dil, L, LANES), bf16),
        scratch_shapes=[pltpu.VMEM((cw // LANES, tm, LANES), f32)],
        compiler_params=_params(2),
        name=f"qkv_proj_d{dil}",
    )(x3, w)


def _softmax_block(qst, kb, vb, bias):
    nq = qst.shape[0] // 2
    s = lax.dot_general(qst, kb, (((1,), (1,)), ((), ())), preferred_element_type=f32) + bias
    m = jnp.max(s, axis=1, keepdims=True)
    p = jnp.exp((s - m).astype(bf16))
    v1 = jnp.concatenate([vb, jnp.ones_like(vb)], axis=1)
    ol = jnp.dot(p, v1, preferred_element_type=f32)
    l = ol[:, LANES:]
    o2 = ol[:, :LANES] * (1.0 / l)
    lse = m + jnp.log(l)
    lane = lax.broadcasted_iota(jnp.int32, (nq, LANES), 1)
    o = jnp.where(lane < HEAD_DIM, o2[:nq], o2[nq:])
    ls = jnp.where(lane < HEAD_DIM, lse[:nq], lse[nq:])
    return o, ls


def _stack_heads(q2):
    lane = lax.broadcasted_iota(jnp.int32, q2.shape, 1)
    zero = jnp.zeros_like(q2)
    return jnp.concatenate([jnp.where(lane < HEAD_DIM, q2, zero),
                            jnp.where(lane >= HEAD_DIM, q2, zero)], axis=0)


def _attn_a_kernel(*refs):
    qkv = refs[:9]
    biases = refs[9:12]
    o_ref = refs[12]
    oscr, lscr = refs[13], refs[14]
    S = o_ref.shape[0]

    def block(it, g):
        dil = DILATIONS[g][1]
        q_ref, k_ref, v_ref = qkv[3 * g:3 * g + 3]
        L = S // dil
        nb = L // QBLK
        r = it // nb
        n = it % nb
        qs = pl.multiple_of(n * QBLK, QBLK)
        ks = pl.multiple_of(jnp.clip(qs - HALF, 0, L - KBLK), HALF)
        var = jnp.where(n == 0, 0, jnp.where(n == nb - 1, 2, 1))
        qst = _stack_heads(q_ref[r, pl.ds(qs, QBLK), :])
        o, ls = _softmax_block(qst, k_ref[r, pl.ds(ks, KBLK), :], v_ref[r, pl.ds(ks, KBLK), :], biases[g][var])
        if dil == 1:
            dst = pl.ds(qs, QBLK)
        else:
            dst = pl.ds(qs * dil + r, QBLK, stride=dil)
        oscr[g, dst, :] = o
        lscr[g, dst, :] = ls

    def body(it, carry):
        for u in range(DIL_BLOCKS_PER_ITER):
            for g in range(len(DILATIONS)):
                block(it * DIL_BLOCKS_PER_ITER + u, g)
        return carry

    lax.fori_loop(0, S // QBLK // DIL_BLOCKS_PER_ITER, body, 0)

    ch = 256

    def merge(c, carry):
        rows = pl.ds(pl.multiple_of(c * ch, ch), ch)
        l0, l1, l2 = lscr[0, rows, :], lscr[1, rows, :], lscr[2, rows, :]
        mx = jnp.maximum(jnp.maximum(l0, l1), l2)
        e0, e1, e2 = jnp.exp(l0 - mx), jnp.exp(l1 - mx), jnp.exp(l2 - mx)
        inv = 1.0 / (e0 + e1 + e2)
        o = (e0 * oscr[0, rows, :] + e1 * oscr[1, rows, :] + e2 * oscr[2, rows, :]) * inv
        o_ref[rows, :] = o.astype(bf16)
        return carry

    lax.fori_loop(0, S // ch, merge, 0)


def _attn_a(qkvs, biases):
    B = qkvs[0].shape[0]
    S = qkvs[0].shape[2] * qkvs[0].shape[3]
    in_specs, args = [], []
    for g, (_, dil) in enumerate(DILATIONS):
        L = S // dil
        for c in range(3):
            in_specs.append(pl.BlockSpec((None, None, dil, L, LANES),
                                         lambda b, hp, c=c: (b, c * HEAD_PAIRS + hp, 0, 0, 0)))
            args.append(qkvs[g])
    for g in range(3):
        in_specs.append(pl.BlockSpec((None, 3, 2 * QBLK, KBLK), lambda b, hp: (hp, 0, 0, 0)))
        args.append(biases[g])
    return pl.pallas_call(
        _attn_a_kernel,
        grid=(B, HEAD_PAIRS),
        in_specs=in_specs,
        out_specs=pl.BlockSpec((None, None, S, LANES), lambda b, hp: (b, hp, 0, 0)),
        out_shape=jax.ShapeDtypeStruct((B, HEAD_PAIRS, S, LANES), bf16),
        scratch_shapes=[pltpu.VMEM((3, S, LANES), f32), pltpu.VMEM((3, S, LANES), f32)],
        compiler_params=_params(2),
        name="dilated_attn",
    )(*args)


def _attn_b_kernel(q_ref, k_ref, v_ref, b_ref, o_ref):
    rows = o_ref.shape[0] // GRID_W
    kh = min(NA_KH, rows)
    nk = kh * GRID_W

    def one_row(i):
        rs = jnp.clip(i - kh // 2, 0, rows - kh)
        var = rs - i + NA_KH - 1 - (NA_KH - kh)
        qs = pl.multiple_of(i * GRID_W, GRID_W)
        ks = pl.multiple_of(rs * GRID_W, GRID_W)
        qst = _stack_heads(q_ref[pl.ds(qs, GRID_W), :])
        o, _ = _softmax_block(qst, k_ref[pl.ds(ks, nk), :], v_ref[pl.ds(ks, nk), :], b_ref[var])
        o_ref[pl.ds(qs, GRID_W), :] = o.astype(bf16)

    def body(it, carry):
        for u in range(NA_ROWS_PER_ITER):
            one_row(it * NA_ROWS_PER_ITER + u)
        return carry

    lax.fori_loop(0, rows // NA_ROWS_PER_ITER, body, 0)


def _attn_b(qkv, bias):
    B, _, _, S, _ = qkv.shape
    nvar, nk = bias.shape[1], bias.shape[3]
    in_specs = [pl.BlockSpec((None, None, None, S, LANES), lambda b, hp, c=c: (b, c * HEAD_PAIRS + hp, 0, 0, 0))
                for c in range(3)]
    in_specs.append(pl.BlockSpec((None, nvar, 2 * GRID_W, nk), lambda b, hp: (hp, 0, 0, 0)))
    return pl.pallas_call(
        _attn_b_kernel,
        grid=(B, HEAD_PAIRS),
        in_specs=in_specs,
        out_specs=pl.BlockSpec((None, None, S, LANES), lambda b, hp: (b, hp, 0, 0)),
        out_shape=jax.ShapeDtypeStruct((B, HEAD_PAIRS, S, LANES), bf16),
        compiler_params=_params(2),
        name="neighbourhood_attn",
    )(qkv, qkv, qkv, bias)


def _layer_norm(y, g, b):
    mu = jnp.mean(y, axis=1, keepdims=True)
    yc = y - mu
    var = jnp.mean(yc * yc, axis=1, keepdims=True)
    return yc * lax.rsqrt(var + LN_EPS) * g + b


def _route(logits):
    shape = logits.shape
    lane = lax.broadcasted_iota(jnp.int32, shape, 1)
    big = jnp.int32(4 * LANES)
    is_g = lane < N_GROUPS
    gl = jnp.where(is_g, logits, NEG_INF)
    mg = jnp.max(gl, axis=1, keepdims=True)
    gsel = jnp.min(jnp.where(gl == mg, lane, big), axis=1, keepdims=True)
    p_sel = 1.0 / jnp.sum(jnp.where(is_g, jnp.exp(gl - mg), 0.0), axis=1, keepdims=True)
    lo = N_GROUPS + gsel * EXPERTS_PER_GROUP
    el = jnp.where((lane >= lo) & (lane < lo + EXPERTS_PER_GROUP), logits, NEG_INF)
    v1 = jnp.max(el, axis=1, keepdims=True)
    i1 = jnp.min(jnp.where(el == v1, lane, big), axis=1, keepdims=True)
    el2 = jnp.where(lane == i1, NEG_INF, el)
    v2 = jnp.max(el2, axis=1, keepdims=True)
    i2 = jnp.min(jnp.where(el2 == v2, lane, big), axis=1, keepdims=True)
    t = jnp.exp(v2 - v1)
    g1 = p_sel / (1.0 + t)
    g2 = p_sel * t / (1.0 + t)
    first_lo = i1 < i2
    a = jnp.where(first_lo, i1, i2) - lo
    b = jnp.where(first_lo, i2, i1) - lo
    ga = jnp.where(first_lo, g1, g2)
    gb = jnp.where(first_lo, g2, g1)
    pair = (EXPERTS_PER_GROUP - 1) * a - ((a * (a - 1)) >> 1) + (b - a - 1)
    cls = (gsel * PAIRS_PER_GROUP + pair).astype(f32)
    return jnp.where(lane == 0, ga, jnp.where(lane == 1, gb, jnp.where(lane == 2, cls, 0.0)))


def _oproj_kernel(att_ref, x_ref, wo_ref, g_ref, b_ref, wr_ref, br_ref, o_ref):
    a = jnp.concatenate([att_ref[h] for h in range(HEAD_PAIRS)], axis=1)
    h = jnp.dot(a, wo_ref[...], preferred_element_type=f32)
    xn = _layer_norm(ALPHA * x_ref[...] + h, g_ref[...], b_ref[...])
    xh = xn.astype(bf16)
    xl = (xn - xh.astype(f32)).astype(bf16)
    hw = jnp.dot(xh, wr_ref[...], preferred_element_type=f32)
    lw = jnp.dot(xl, wr_ref[:, :LANES], preferred_element_type=f32)
    logits = hw[:, :LANES] + hw[:, LANES:] + lw + br_ref[...]
    o_ref[:, :D_MODEL] = xn
    o_ref[:, D_MODEL:] = _route(logits)


def _oproj_ln_route(att, x, wo, g, b, wr, br):
    B, _, S, _ = att.shape
    T = B * S
    tm = TOK_TM
    per = S // tm
    vec = lambda n: pl.BlockSpec((1, n), lambda i: (0, 0))
    return pl.pallas_call(
        _oproj_kernel,
        grid=(T // tm,),
        in_specs=[pl.BlockSpec((None, HEAD_PAIRS, tm, LANES), lambda i: (i // per, 0, i % per, 0)),
                  pl.BlockSpec((tm, D_MODEL), lambda i: (i, 0)),
                  pl.BlockSpec((D_MODEL, D_MODEL), lambda i: (0, 0)),
                  vec(D_MODEL), vec(D_MODEL),
                  pl.BlockSpec((D_MODEL, 2 * LANES), lambda i: (0, 0)), vec(LANES)],
        out_specs=pl.BlockSpec((tm, ROW_W), lambda i: (i, 0)),
        out_shape=jax.ShapeDtypeStruct((T, ROW_W), f32),
        compiler_params=_params(1),
        name="oproj_ln_route",
    )(att, x, wo, g, b, wr, br)


def _for_rows(n, fn):
    def body(c, carry):
        for u in range(DMA_UNROLL):
            fn(c * DMA_UNROLL + u)
        return carry

    lax.fori_loop(0, n // DMA_UNROLL, body, 0)


def _dispatch_kernel(pos_ref, rows_ref, init_ref, xs_ref, sem):
    del init_ref

    def copy(k):
        return pltpu.make_async_copy(rows_ref.at[pl.ds(k, 1), :], xs_ref.at[pl.ds(pos_ref[0, 0, k], 1), :], sem)

    n = rows_ref.shape[0]
    _for_rows(n, lambda k: copy(k).start())
    _for_rows(n, lambda k: copy(k).wait())


def _dispatch(rows, pos3, slots):
    T = rows.shape[0]
    tm = pos3.shape[2]
    return pl.pallas_call(
        _dispatch_kernel,
        grid=(T // tm,),
        in_specs=[pl.BlockSpec((1, 1, tm), lambda i: (i, 0, 0), memory_space=pltpu.SMEM),
                  pl.BlockSpec((tm, ROW_W), lambda i: (i, 0)),
                  pl.BlockSpec(memory_space=pl.ANY)],
        out_specs=pl.BlockSpec(memory_space=pl.ANY),
        out_shape=jax.ShapeDtypeStruct(slots.shape, f32),
        scratch_shapes=[pltpu.SemaphoreType.DMA(())],
        input_output_aliases={2: 0},
        compiler_params=_params(1),
        name="moe_dispatch",
    )(pos3, rows, slots)


def _swiglu(xt, w1, w3, w2):
    h1 = jnp.dot(xt, w1, preferred_element_type=f32)
    h3 = jnp.dot(xt, w3, preferred_element_type=f32)
    h = h1 * (1.0 / (1.0 + jnp.exp(-h1))) * h3
    return jnp.dot(h.astype(bf16), w2, preferred_element_type=f32)


def _moe_kernel(ea_ref, eb_ref, valid_ref, xs_ref, w1a, w3a, w2a, w1b, w3b, w2b, ys_ref):
    i = pl.program_id(0)

    @pl.when(valid_ref[i] != 0)
    def _():
        xt = xs_ref[:, :D_MODEL].astype(bf16)
        ga = xs_ref[:, D_MODEL:D_MODEL + 1]
        gb = xs_ref[:, D_MODEL + 1:D_MODEL + 2]
        ys_ref[...] = (ga * _swiglu(xt, w1a[...], w3a[...], w2a[...])
                       + gb * _swiglu(xt, w1b[...], w3b[...], w2b[...]))

    @pl.when(valid_ref[i] == 0)
    def _():
        ys_ref[...] = jnp.zeros_like(ys_ref)


def _moe(xs, tile_a, tile_b, tile_valid, w1, w3, w2, layer):
    n_slots = xs.shape[0]
    tm = MOE_TM
    wa = lambda shape: pl.BlockSpec((None, None) + shape, lambda i, ea, eb, va: (layer, ea[i], 0, 0))
    wb = lambda shape: pl.BlockSpec((None, None) + shape, lambda i, ea, eb, va: (layer, eb[i], 0, 0))
    up, down = (D_MODEL, D_EXPERT), (D_EXPERT, D_MODEL)
    grid_spec = pltpu.PrefetchScalarGridSpec(
        num_scalar_prefetch=3,
        grid=(n_slots // tm,),
        in_specs=[pl.BlockSpec((tm, ROW_W), lambda i, ea, eb, va: (i, 0)),
                  wa(up), wa(up), wa(down), wb(up), wb(up), wb(down)],
        out_specs=pl.BlockSpec((tm, D_MODEL), lambda i, ea, eb, va: (i, 0)),
    )
    return pl.pallas_call(
        _moe_kernel,
        grid_spec=grid_spec,
        out_shape=jax.ShapeDtypeStruct((n_slots, D_MODEL), f32),
        compiler_params=_params(1),
        name="moe_experts",
    )(tile_a, tile_b, tile_valid, xs, w1, w3, w2, w1, w3, w2)


def _combine_kernel(pos_ref, pos_next_ref, x_ref, ys_ref, g_ref, b_ref, o_ref, buf, sem):
    i = pl.program_id(0)
    n = x_ref.shape[0]
    slot = i % 2

    def copy(p_ref, s):
        return lambda k: pltpu.make_async_copy(ys_ref.at[pl.ds(p_ref[0, 0, k], 1), :],
                                               buf.at[s, pl.ds(k, 1), :], sem.at[s])

    @pl.when(i == 0)
    def _():
        _for_rows(n, lambda k: copy(pos_ref, slot)(k).start())

    @pl.when(i + 1 < pl.num_programs(0))
    def _():
        _for_rows(n, lambda k: copy(pos_next_ref, 1 - slot)(k).start())

    _for_rows(n, lambda k: copy(pos_ref, slot)(k).wait())
    o_ref[...] = _layer_norm(ALPHA * x_ref[...] + buf[slot], g_ref[...], b_ref[...])


def _combine_ln(rows, ys, pos3, g, b):
    T = rows.shape[0]
    nt, _, tm = pos3.shape
    vec = pl.BlockSpec((1, D_MODEL), lambda i: (0, 0))
    return pl.pallas_call(
        _combine_kernel,
        grid=(nt,),
        in_specs=[pl.BlockSpec((1, 1, tm), lambda i: (i, 0, 0), memory_space=pltpu.SMEM),
                  pl.BlockSpec((1, 1, tm), lambda i: (jnp.minimum(i + 1, nt - 1), 0, 0), memory_space=pltpu.SMEM),
                  pl.BlockSpec((tm, D_MODEL), lambda i: (i, 0)),
                  pl.BlockSpec(memory_space=pl.ANY), vec, vec],
        out_specs=pl.BlockSpec((tm, D_MODEL), lambda i: (i, 0)),
        out_shape=jax.ShapeDtypeStruct((T, D_MODEL), f32),
        scratch_shapes=[pltpu.VMEM((2, tm, D_MODEL), f32), pltpu.SemaphoreType.DMA((2,))],
        compiler_params=_params(1),
        name="moe_combine_ln",
    )(pos3, pos3, rows, ys, g, b)


def _plan(cls, n_tiles):
    onehot = (cls[:, None] == jnp.arange(N_CLASSES, dtype=jnp.int32)[None, :]).astype(jnp.int32)
    csum = jnp.cumsum(onehot, axis=0)
    rank = jnp.sum(csum * onehot, axis=1) - 1
    counts = csum[-1]
    tiles = (counts + MOE_TM - 1) // MOE_TM
    tile_end = jnp.cumsum(tiles)
    tile_start = tile_end - tiles
    pos = jnp.sum(onehot * tile_start[None, :], axis=1) * MOE_TM + rank
    tile_id = jnp.arange(n_tiles, dtype=jnp.int32)
    used = tile_end[-1]
    tcls = jnp.searchsorted(tile_end, jnp.minimum(tile_id, used - 1), side="right").astype(jnp.int32)
    grp = tcls // PAIRS_PER_GROUP
    pair = tcls % PAIRS_PER_GROUP
    pa, pb = np.triu_indices(EXPERTS_PER_GROUP, 1)
    ea = grp * EXPERTS_PER_GROUP + jnp.asarray(pa, jnp.int32)[pair]
    eb = grp * EXPERTS_PER_GROUP + jnp.asarray(pb, jnp.int32)[pair]
    valid = (tile_id < used).astype(jnp.int32)
    return pos.astype(jnp.int32), ea, eb, valid


def _t5_buckets(rel):
    nb = N_BUCKETS // 2
    max_exact = nb // 2
    ret = np.where(rel > 0, nb, 0)
    n = np.abs(rel)
    nf = np.maximum(n, 1).astype(np.float32)
    large = max_exact + (np.log(nf / np.float32(max_exact)) / np.float32(math.log(T5_MAX_DISTANCE / max_exact))
                         * np.float32(nb - max_exact)).astype(np.int32)
    large = np.minimum(large, nb - 1)
    return ret + np.where(n < max_exact, n, large)


def _table_lookup(table, idx):
    onehot = (jnp.asarray(idx, jnp.int32)[..., None] == jnp.arange(table.shape[0], dtype=jnp.int32)).astype(f32)
    return jnp.einsum("...n,nh->...h", onehot, table, precision=lax.Precision.HIGHEST)


def _dilated_bias(t5_table, dil):
    qi = np.arange(QBLK)[:, None]
    kj = np.arange(KBLK)[None, :]
    out = []
    for shift in (0, HALF, 2 * HALF):
        rel = kj - qi - shift
        band = np.abs(rel) <= HALF
        vals = _table_lookup(t5_table.astype(f32), _t5_buckets(rel * dil))
        vals = jnp.where(jnp.asarray(band)[:, :, None], vals, NEG_INF)
        out.append(vals.transpose(2, 0, 1))
    tab = jnp.stack(out, axis=1)
    tab = tab.reshape(HEAD_PAIRS, 2, 3, QBLK, KBLK).transpose(0, 2, 1, 3, 4)
    return tab.reshape(HEAD_PAIRS, 3, 2 * QBLK, KBLK)


def _neighbourhood_bias(rpb, rows):
    kh = min(NA_KH, rows)
    nvar = NA_KH if rows > kh else 1
    j = np.arange(GRID_W)[:, None]
    kc = np.arange(GRID_W)[None, :]
    ws = np.clip(j - NA_KW // 2, 0, GRID_W - NA_KW)
    inwin = (kc >= ws) & (kc < ws + NA_KW)
    dc = np.clip(kc - j + NA_KW - 1, 0, 2 * NA_KW - 2)
    ndr, ndc = rpb.shape[1], rpb.shape[2]
    cols = _table_lookup(rpb.astype(f32).reshape(N_HEADS * ndr, ndc).T, dc)
    cols = jnp.where(jnp.asarray(inwin)[:, :, None], cols, NEG_INF)
    cols = cols.reshape(GRID_W, GRID_W, N_HEADS, ndr).transpose(2, 3, 0, 1)
    out = []
    for var in range(nvar):
        lo = var + (NA_KH - kh)
        vals = cols[:, lo:lo + kh]
        out.append(vals.transpose(0, 2, 1, 3).reshape(N_HEADS, GRID_W, kh * GRID_W))
    tab = jnp.stack(out, axis=1)
    tab = tab.reshape(HEAD_PAIRS, 2, nvar, GRID_W, kh * GRID_W).transpose(0, 2, 1, 3, 4)
    return tab.reshape(HEAD_PAIRS, nvar, 2 * GRID_W, kh * GRID_W)


def _qkv_weight(w):
    hd = N_HEADS * HEAD_DIM
    scale = jnp.concatenate([jnp.full((hd,), HEAD_DIM ** -0.5, f32), jnp.ones((2 * hd,), f32)])
    return (w * scale[None, :]).astype(bf16)


def _router_weight(w_rg, b_rg, w_re, b_re):
    pad = LANES - N_GROUPS - N_EXPERTS
    wr = jnp.concatenate([w_rg, w_re, jnp.zeros((D_MODEL, pad), f32)], axis=1)
    br = jnp.concatenate([b_rg, b_re, jnp.zeros((pad,), f32)])[None, :]
    hi = wr.astype(bf16)
    lo = (wr - hi.astype(f32)).astype(bf16)
    return jnp.concatenate([hi, lo], axis=1), br


def _trunk(x3, t5_table, w_qkv_a, w_o_a, w_qkv_b, w_o_b, rpb_b, ln_g, ln_b,
           w_rg, b_rg, w_re, b_re, w1, w3, w2):
    B, S, D = x3.shape
    T = B * S
    n_slots = T + N_CLASSES * MOE_TM
    n_tiles = n_slots // MOE_TM
    hd = N_HEADS * HEAD_DIM
    x = x3.reshape(T, D)
    w1b, w3b, w2b = w1.astype(bf16), w3.astype(bf16), w2.astype(bf16)
    bias_a = [_dilated_bias(t5_table, dil) for _, dil in DILATIONS]
    xs = jnp.zeros((n_slots, ROW_W), f32)
    for i in range(DEPTH):
        j = i // 2
        x3 = x.reshape(B, S, D)
        if i % 2 == 0:
            qkvs = [_qkv_proj(x3, _qkv_weight(w_qkv_a[j][:, g * 3 * hd:(g + 1) * 3 * hd]), dil)
                    for g, (_, dil) in enumerate(DILATIONS)]
            att = _attn_a(qkvs, bias_a)
            wo = w_o_a[j]
        else:
            qkv = _qkv_proj(x3, _qkv_weight(w_qkv_b[j]), 1)
            att = _attn_b(qkv, _neighbourhood_bias(rpb_b[j], S // GRID_W))
            wo = w_o_b[j]
        wr, br = _router_weight(w_rg[i], b_rg[i], w_re[i], b_re[i])
        rows = _oproj_ln_route(att, x, wo.astype(bf16), ln_g[i, 0][None, :], ln_b[i, 0][None, :], wr, br)
        cls = rows[:, D_MODEL + 2].astype(jnp.int32)
        pos, ea, eb, valid = _plan(cls, n_tiles)
        pos3 = pos.reshape(T // ROW_DMA_TM, 1, ROW_DMA_TM)
        xs = _dispatch(rows, pos3, xs)
        ys = _moe(xs, ea, eb, valid, w1b, w3b, w2b, i)
        x = _combine_ln(rows, ys, pos3, ln_g[i, 1][None, :], ln_b[i, 1][None, :])
    return x.reshape(B, S, D)


def kernel(x_prompt, x_sample, t5_table, w_qkv_a, w_o_a, w_qkv_b, w_o_b, rpb_b, ln_g, ln_b,
           w_rg, b_rg, w_re, b_re, w1, w3, w2):
    nb = x_prompt.shape[0]
    x = jnp.concatenate([x_prompt, x_sample], axis=0)
    y = _trunk(x, t5_table, w_qkv_a, w_o_a, w_qkv_b, w_o_b, rpb_b, ln_g, ln_b,
               w_rg, b_rg, w_re, b_re, w1, w3, w2)
    return (y[:nb], y[nb:])
```

```python
import functools
import math

import numpy as np
import jax
import jax.numpy as jnp
from jax import lax
from jax.experimental import pallas as pl
from jax.experimental.pallas import tpu as pltpu

D_MODEL = 1024
SEQ = 4096
DEPTH = 4
HEAD_DIM = 64
N_HEADS = 16
DILATIONS = ((128, 1), (512, 4), (2048, 16))
N_BUCKETS = 32
T5_MAX_DISTANCE = 1024
GRID_W = 64
NA_KH = 8
NA_KW = 16
N_GROUPS = 4
EXPERTS_PER_GROUP = 8
N_EXPERTS = N_GROUPS * EXPERTS_PER_GROUP
D_EXPERT = 512
ALPHA = (2 * DEPTH) ** 0.25
LN_EPS = 1e-5
NEG_INF = -1e30

LANES = 128
SUBLANES = 8
HEAD_PAIRS = N_HEADS * HEAD_DIM // LANES
QKV_BLOCKS = 3 * HEAD_PAIRS
VMEM_LIMIT = 56 * 1024 * 1024

QBLK = 128
KBLK = 256
HALF = 64
NA_ROWS_PER_ITER = 32
DIL_BLOCKS_PER_ITER = 16
PAIRS_PER_GROUP = EXPERTS_PER_GROUP * (EXPERTS_PER_GROUP - 1) // 2
N_CLASSES = N_GROUPS * PAIRS_PER_GROUP
ROW_W = D_MODEL + LANES
MOE_TM = 256
TOK_TM = 1024
ROW_DMA_TM = 1024
DMA_UNROLL = 8

f32 = jnp.float32
bf16 = jnp.bfloat16


def _params(n_grid):
    return pltpu.CompilerParams(dimension_semantics=("arbitrary",) * n_grid,
                                vmem_limit_bytes=VMEM_LIMIT)


def _qkv_kernel(x_ref, w_ref, o_ref, acc_ref, *, dil, tm, cw):
    nsub = cw // LANES
    rows = tm // dil
    rows_in = len(x_ref.shape) == 3
    if rows_in:
        xb = jnp.concatenate([x_ref[:, r, :] for r in range(dil)], axis=0).astype(bf16)
    else:
        xb = x_ref[...].astype(bf16)
    for j in range(QKV_BLOCKS // nsub):
        res = jnp.dot(xb, w_ref[:, j * cw:(j + 1) * cw], preferred_element_type=f32)
        if dil == 1 or rows_in:
            for kk in range(nsub):
                for r in range(dil):
                    o_ref[j * nsub + kk, r, :, :] = res[r * rows:(r + 1) * rows,
                                                        kk * LANES:(kk + 1) * LANES].astype(bf16)
            continue
        for kk in range(nsub):
            acc_ref[kk] = res[:, kk * LANES:(kk + 1) * LANES]
        for kk in range(nsub):
            for r in range(dil):
                o_ref[j * nsub + kk, r, :, :] = acc_ref[kk, pl.ds(r, rows, stride=dil), :].astype(bf16)


def _qkv_proj(x3, w, dil):
    B, S, D = x3.shape
    tm, cw = 1024, 512
    L = S // dil
    if dil % SUBLANES == 0:
        x3 = x3.reshape(B, L, dil, D)
        x_spec = pl.BlockSpec((None, tm // dil, dil, D), lambda b, i: (b, i, 0, 0))
    else:
        x_spec = pl.BlockSpec((None, tm, D), lambda b, i: (b, i, 0))
    return pl.pallas_call(
        functools.partial(_qkv_kernel, dil=dil, tm=tm, cw=cw),
        grid=(B, S // tm),
        in_specs=[x_spec,
                  pl.BlockSpec((D, QKV_BLOCKS * LANES), lambda b, i: (0, 0))],
        out_specs=pl.BlockSpec((None, QKV_BLOCKS, dil, tm // dil, LANES), lambda b, i: (b, 0, 0, i, 0)),
        out_shape=jax.ShapeDtypeStruct((B, QKV_BLOCKS, dil, L, LANES), bf16),
        scratch_shapes=[pltpu.VMEM((cw // LANES, tm, LANES), f32)],
        compiler_params=_params(2),
        name=f"qkv_proj_d{dil}",
    )(x3, w)


def _softmax_block(qst, kb, vb, bias):
    nq = qst.shape[0] // 2
    s = lax.dot_general(qst, kb, (((1,), (1,)), ((), ())), preferred_element_type=f32) + bias
    m = jnp.max(s, axis=1, keepdims=True)
    p = jnp.exp((s - m).astype(bf16))
    v1 = jnp.concatenate([vb, jnp.ones_like(vb)], axis=1)
    ol = jnp.dot(p, v1, preferred_element_type=f32)
    l = ol[:, LANES:]
    o2 = ol[:, :LANES] * (1.0 / l)
    lse = m + jnp.log(l)
    lane = lax.broadcasted_iota(jnp.int32, (nq, LANES), 1)
    o = jnp.where(lane < HEAD_DIM, o2[:nq], o2[nq:])
    ls = jnp.where(lane < HEAD_DIM, lse[:nq], lse[nq:])
    return o, ls


def _stack_heads(q2):
    lane = lax.broadcasted_iota(jnp.int32, q2.shape, 1)
    zero = jnp.zeros_like(q2)
    return jnp.concatenate([jnp.where(lane < HEAD_DIM, q2, zero),
                            jnp.where(lane >= HEAD_DIM, q2, zero)], axis=0)


def _attn_a_kernel(*refs):
    qkv = refs[:9]
    biases = refs[9:12]
    o_ref = refs[12]
    oscr, lscr = refs[13], refs[14]
    S = o_ref.shape[0]

    def block(it, g):
        dil = DILATIONS[g][1]
        q_ref, k_ref, v_ref = qkv[3 * g:3 * g + 3]
        L = S // dil
        nb = L // QBLK
        r = it // nb
        n = it % nb
        qs = pl.multiple_of(n * QBLK, QBLK)
        ks = pl.multiple_of(jnp.clip(qs - HALF, 0, L - KBLK), HALF)
        var = jnp.where(n == 0, 0, jnp.where(n == nb - 1, 2, 1))
        qst = _stack_heads(q_ref[r, pl.ds(qs, QBLK), :])
        o, ls = _softmax_block(qst, k_ref[r, pl.ds(ks, KBLK), :], v_ref[r, pl.ds(ks, KBLK), :], biases[g][var])
        if dil == 1:
            dst = pl.ds(qs, QBLK)
        else:
            dst = pl.ds(qs * dil + r, QBLK, stride=dil)
        oscr[g, dst, :] = o
        lscr[g, dst, :] = ls

    def body(it, carry):
        for u in range(DIL_BLOCKS_PER_ITER):
            for g in range(len(DILATIONS)):
                block(it * DIL_BLOCKS_PER_ITER + u, g)
        return carry

    lax.fori_loop(0, S // QBLK // DIL_BLOCKS_PER_ITER, body, 0)

    ch = 256

    def merge(c, carry):
        rows = pl.ds(pl.multiple_of(c * ch, ch), ch)
        l0, l1, l2 = lscr[0, rows, :], lscr[1, rows, :], lscr[2, rows, :]
        mx = jnp.maximum(jnp.maximum(l0, l1), l2)
        e0, e1, e2 = jnp.exp(l0 - mx), jnp.exp(l1 - mx), jnp.exp(l2 - mx)
        inv = 1.0 / (e0 + e1 + e2)
        o = (e0 * oscr[0, rows, :] + e1 * oscr[1, rows, :] + e2 * oscr[2, rows, :]) * inv
        o_ref[rows, :] = o.astype(bf16)
        return carry

    lax.fori_loop(0, S // ch, merge, 0)


def _attn_a(qkvs, biases):
    B = qkvs[0].shape[0]
    S = qkvs[0].shape[2] * qkvs[0].shape[3]
    in_specs, args = [], []
    for g, (_, dil) in enumerate(DILATIONS):
        L = S // dil
        for c in range(3):
            in_specs.append(pl.BlockSpec((None, None, dil, L, LANES),
                                         lambda b, hp, c=c: (b, c * HEAD_PAIRS + hp, 0, 0, 0)))
            args.append(qkvs[g])
    for g in range(3):
        in_specs.append(pl.BlockSpec((None, 3, 2 * QBLK, KBLK), lambda b, hp: (hp, 0, 0, 0)))
        args.append(biases[g])
    return pl.pallas_call(
        _attn_a_kernel,
        grid=(B, HEAD_PAIRS),
        in_specs=in_specs,
        out_specs=pl.BlockSpec((None, None, S, LANES), lambda b, hp: (b, hp, 0, 0)),
        out_shape=jax.ShapeDtypeStruct((B, HEAD_PAIRS, S, LANES), bf16),
        scratch_shapes=[pltpu.VMEM((3, S, LANES), f32), pltpu.VMEM((3, S, LANES), f32)],
        compiler_params=_params(2),
        name="dilated_attn",
    )(*args)


def _attn_b_kernel(q_ref, k_ref, v_ref, b_ref, o_ref):
    rows = o_ref.shape[0] // GRID_W
    kh = min(NA_KH, rows)
    nk = kh * GRID_W

    def one_row(i):
        rs = jnp.clip(i - kh // 2, 0, rows - kh)
        var = rs - i + NA_KH - 1 - (NA_KH - kh)
        qs = pl.multiple_of(i * GRID_W, GRID_W)
        ks = pl.multiple_of(rs * GRID_W, GRID_W)
        qst = _stack_heads(q_ref[pl.ds(qs, GRID_W), :])
        o, _ = _softmax_block(qst, k_ref[pl.ds(ks, nk), :], v_ref[pl.ds(ks, nk), :], b_ref[var])
        o_ref[pl.ds(qs, GRID_W), :] = o.astype(bf16)

    def body(it, carry):
        for u in range(NA_ROWS_PER_ITER):
            one_row(it * NA_ROWS_PER_ITER + u)
        return carry

    lax.fori_loop(0, rows // NA_ROWS_PER_ITER, body, 0)


def _attn_b(qkv, bias):
    B, _, _, S, _ = qkv.shape
    nvar, nk = bias.shape[1], bias.shape[3]
    in_specs = [pl.BlockSpec((None, None, None, S, LANES), lambda b, hp, c=c: (b, c * HEAD_PAIRS + hp, 0, 0, 0))
                for c in range(3)]
    in_specs.append(pl.BlockSpec((None, nvar, 2 * GRID_W, nk), lambda b, hp: (hp, 0, 0, 0)))
    return pl.pallas_call(
        _attn_b_kernel,
        grid=(B, HEAD_PAIRS),
        in_specs=in_specs,
        out_specs=pl.BlockSpec((None, None, S, LANES), lambda b, hp: (b, hp, 0, 0)),
        out_shape=jax.ShapeDtypeStruct((B, HEAD_PAIRS, S, LANES), bf16),
        compiler_params=_params(2),
        name="neighbourhood_attn",
    )(qkv, qkv, qkv, bias)


def _layer_norm(y, g, b):
    mu = jnp.mean(y, axis=1, keepdims=True)
    yc = y - mu
    var = jnp.mean(yc * yc, axis=1, keepdims=True)
    return yc * lax.rsqrt(var + LN_EPS) * g + b


def _route(logits):
    shape = logits.shape
    lane = lax.broadcasted_iota(jnp.int32, shape, 1)
    big = jnp.int32(4 * LANES)
    is_g = lane < N_GROUPS
    gl = jnp.where(is_g, logits, NEG_INF)
    mg = jnp.max(gl, axis=1, keepdims=True)
    gsel = jnp.min(jnp.where(gl == mg, lane, big), axis=1, keepdims=True)
    p_sel = 1.0 / jnp.sum(jnp.where(is_g, jnp.exp(gl - mg), 0.0), axis=1, keepdims=True)
    lo = N_GROUPS + gsel * EXPERTS_PER_GROUP
    el = jnp.where((lane >= lo) & (lane < lo + EXPERTS_PER_GROUP), logits, NEG_INF)
    v1 = jnp.max(el, axis=1, keepdims=True)
    i1 = jnp.min(jnp.where(el == v1, lane, big), axis=1, keepdims=True)
    el2 = jnp.where(lane == i1, NEG_INF, el)
    v2 = jnp.max(el2, axis=1, keepdims=True)
    i2 = jnp.min(jnp.where(el2 == v2, lane, big), axis=1, keepdims=True)
    t = jnp.exp(v2 - v1)
    g1 = p_sel / (1.0 + t)
    g2 = p_sel * t / (1.0 + t)
    first_lo = i1 < i2
    a = jnp.where(first_lo, i1, i2) - lo
    b = jnp.where(first_lo, i2, i1) - lo
    ga = jnp.where(first_lo, g1, g2)
    gb = jnp.where(first_lo, g2, g1)
    pair = (EXPERTS_PER_GROUP - 1) * a - ((a * (a - 1)) >> 1) + (b - a - 1)
    cls = (gsel * PAIRS_PER_GROUP + pair).astype(f32)
    return jnp.where(lane == 0, ga, jnp.where(lane == 1, gb, jnp.where(lane == 2, cls, 0.0)))


def _oproj_kernel(att_ref, x_ref, wo_ref, g_ref, b_ref, wr_ref, br_ref, o_ref):
    a = jnp.concatenate([att_ref[h] for h in range(HEAD_PAIRS)], axis=1)
    h = jnp.dot(a, wo_ref[...], preferred_element_type=f32)
    xn = _layer_norm(ALPHA * x_ref[...] + h, g_ref[...], b_ref[...])
    xh = xn.astype(bf16)
    xl = (xn - xh.astype(f32)).astype(bf16)
    hw = jnp.dot(xh, wr_ref[...], preferred_element_type=f32)
    lw = jnp.dot(xl, wr_ref[:, :LANES], preferred_element_type=f32)
    logits = hw[:, :LANES] + hw[:, LANES:] + lw + br_ref[...]
    o_ref[:, :D_MODEL] = xn
    o_ref[:, D_MODEL:] = _route(logits)


def _oproj_ln_route(att, x, wo, g, b, wr, br):
    B, _, S, _ = att.shape
    T = B * S
    tm = TOK_TM
    per = S // tm
    vec = lambda n: pl.BlockSpec((1, n), lambda i: (0, 0))
    return pl.pallas_call(
        _oproj_kernel,
        grid=(T // tm,),
        in_specs=[pl.BlockSpec((None, HEAD_PAIRS, tm, LANES), lambda i: (i // per, 0, i % per, 0)),
                  pl.BlockSpec((tm, D_MODEL), lambda i: (i, 0)),
                  pl.BlockSpec((D_MODEL, D_MODEL), lambda i: (0, 0)),
                  vec(D_MODEL), vec(D_MODEL),
                  pl.BlockSpec((D_MODEL, 2 * LANES), lambda i: (0, 0)), vec(LANES)],
        out_specs=pl.BlockSpec((tm, ROW_W), lambda i: (i, 0)),
        out_shape=jax.ShapeDtypeStruct((T, ROW_W), f32),
        compiler_params=_params(1),
        name="oproj_ln_route",
    )(att, x, wo, g, b, wr, br)


def _for_rows(n, fn):
    def body(c, carry):
        base = pl.multiple_of(c * DMA_UNROLL, DMA_UNROLL)
        for u in range(DMA_UNROLL):
            fn(base, u)
        return carry

    lax.fori_loop(0, n // DMA_UNROLL, body, 0)


def _tile_row(ref, base, u):
    return ref.at[pl.ds(base, DMA_UNROLL), :].at[pl.ds(u, 1), :]


def _dispatch_kernel(pos_ref, rows_ref, init_ref, xs_ref, sem):
    del init_ref

    def copy(base, u):
        dst = xs_ref.at[pl.ds(pos_ref[0, 0, base + u], 1), :]
        return pltpu.make_async_copy(_tile_row(rows_ref, base, u), dst, sem)

    n = rows_ref.shape[0]
    _for_rows(n, lambda base, u: copy(base, u).start())
    _for_rows(n, lambda base, u: copy(base, u).wait())


def _dispatch(rows, pos3, slots):
    T = rows.shape[0]
    tm = pos3.shape[2]
    return pl.pallas_call(
        _dispatch_kernel,
        grid=(T // tm,),
        in_specs=[pl.BlockSpec((1, 1, tm), lambda i: (i, 0, 0), memory_space=pltpu.SMEM),
                  pl.BlockSpec((tm, ROW_W), lambda i: (i, 0)),
                  pl.BlockSpec(memory_space=pl.ANY)],
        out_specs=pl.BlockSpec(memory_space=pl.ANY),
        out_shape=jax.ShapeDtypeStruct(slots.shape, f32),
        scratch_shapes=[pltpu.SemaphoreType.DMA(())],
        input_output_aliases={2: 0},
        compiler_params=_params(1),
        name="moe_dispatch",
    )(pos3, rows, slots)


def _swiglu(xt, w1, w3, w2):
    h1 = jnp.dot(xt, w1, preferred_element_type=f32)
    h3 = jnp.dot(xt, w3, preferred_element_type=f32)
    h = h1 * (1.0 / (1.0 + jnp.exp(-h1))) * h3
    return jnp.dot(h.astype(bf16), w2, preferred_element_type=f32)


def _moe_kernel(grp_ref, la_ref, lb_ref, valid_ref, xs_ref, w1_ref, w3_ref, w2_ref, ys_ref):
    del grp_ref
    i = pl.program_id(0)

    @pl.when(valid_ref[i] != 0)
    def _():
        a, b = la_ref[i], lb_ref[i]
        xt = xs_ref[:, :D_MODEL].astype(bf16)
        ga = xs_ref[:, D_MODEL:D_MODEL + 1]
        gb = xs_ref[:, D_MODEL + 1:D_MODEL + 2]
        ys_ref[...] = (ga * _swiglu(xt, w1_ref[a], w3_ref[a], w2_ref[a])
                       + gb * _swiglu(xt, w1_ref[b], w3_ref[b], w2_ref[b]))

    @pl.when(valid_ref[i] == 0)
    def _():
        ys_ref[...] = jnp.zeros_like(ys_ref)


def _moe(xs, tile_grp, tile_a, tile_b, tile_valid, w1, w3, w2, layer):
    n_slots = xs.shape[0]
    tm = MOE_TM
    group_w = lambda shape: pl.BlockSpec((None, None, EXPERTS_PER_GROUP) + shape,
                                         lambda i, g, a, b, va: (layer, g[i], 0, 0, 0),
                                         pipeline_mode=pl.Buffered(1))
    up, down = (D_MODEL, D_EXPERT), (D_EXPERT, D_MODEL)
    grid_spec = pltpu.PrefetchScalarGridSpec(
        num_scalar_prefetch=4,
        grid=(n_slots // tm,),
        in_specs=[pl.BlockSpec((tm, ROW_W), lambda i, g, a, b, va: (i, 0)),
                  group_w(up), group_w(up), group_w(down)],
        out_specs=pl.BlockSpec((tm, D_MODEL), lambda i, g, a, b, va: (i, 0)),
    )
    return pl.pallas_call(
        _moe_kernel,
        grid_spec=grid_spec,
        out_shape=jax.ShapeDtypeStruct((n_slots, D_MODEL), f32),
        compiler_params=_params(1),
        name="moe_experts",
    )(tile_grp, tile_a, tile_b, tile_valid, xs, w1, w3, w2)


def _combine_kernel(pos_ref, pos_next_ref, x_ref, ys_ref, g_ref, b_ref, o_ref, buf, sem):
    i = pl.program_id(0)
    n = x_ref.shape[0]
    slot = i % 2

    def copy(p_ref, s):
        def make(base, u):
            src = ys_ref.at[pl.ds(p_ref[0, 0, base + u], 1), :]
            return pltpu.make_async_copy(src, _tile_row(buf.at[s], base, u), sem.at[s])
        return make

    @pl.when(i == 0)
    def _():
        _for_rows(n, lambda base, u: copy(pos_ref, slot)(base, u).start())

    @pl.when(i + 1 < pl.num_programs(0))
    def _():
        _for_rows(n, lambda base, u: copy(pos_next_ref, 1 - slot)(base, u).start())

    _for_rows(n, lambda base, u: copy(pos_ref, slot)(base, u).wait())
    o_ref[...] = _layer_norm(ALPHA * x_ref[...] + buf[slot], g_ref[...], b_ref[...])


def _combine_ln(rows, ys, pos3, g, b):
    T = rows.shape[0]
    nt, _, tm = pos3.shape
    vec = pl.BlockSpec((1, D_MODEL), lambda i: (0, 0))
    return pl.pallas_call(
        _combine_kernel,
        grid=(nt,),
        in_specs=[pl.BlockSpec((1, 1, tm), lambda i: (i, 0, 0), memory_space=pltpu.SMEM),
                  pl.BlockSpec((1, 1, tm), lambda i: (jnp.minimum(i + 1, nt - 1), 0, 0), memory_space=pltpu.SMEM),
                  pl.BlockSpec((tm, D_MODEL), lambda i: (i, 0)),
                  pl.BlockSpec(memory_space=pl.ANY), vec, vec],
        out_specs=pl.BlockSpec((tm, D_MODEL), lambda i: (i, 0)),
        out_shape=jax.ShapeDtypeStruct((T, D_MODEL), f32),
        scratch_shapes=[pltpu.VMEM((2, tm, D_MODEL), f32), pltpu.SemaphoreType.DMA((2,))],
        compiler_params=_params(1),
        name="moe_combine_ln",
    )(pos3, pos3, rows, ys, g, b)


def _plan(cls, n_tiles):
    onehot = (cls[:, None] == jnp.arange(N_CLASSES, dtype=jnp.int32)[None, :]).astype(jnp.int32)
    csum = jnp.cumsum(onehot, axis=0)
    rank = jnp.sum(csum * onehot, axis=1) - 1
    counts = csum[-1]
    tiles = (counts + MOE_TM - 1) // MOE_TM
    tile_end = jnp.cumsum(tiles)
    tile_start = tile_end - tiles
    pos = jnp.sum(onehot * tile_start[None, :], axis=1) * MOE_TM + rank
    tile_id = jnp.arange(n_tiles, dtype=jnp.int32)
    used = tile_end[-1]
    tcls = jnp.searchsorted(tile_end, jnp.minimum(tile_id, used - 1), side="right").astype(jnp.int32)
    grp = tcls // PAIRS_PER_GROUP
    pair = tcls % PAIRS_PER_GROUP
    pa, pb = np.triu_indices(EXPERTS_PER_GROUP, 1)
    la = jnp.asarray(pa, jnp.int32)[pair]
    lb = jnp.asarray(pb, jnp.int32)[pair]
    valid = (tile_id < used).astype(jnp.int32)
    return pos.astype(jnp.int32), grp, la, lb, valid


def _t5_buckets(rel):
    nb = N_BUCKETS // 2
    max_exact = nb // 2
    ret = np.where(rel > 0, nb, 0)
    n = np.abs(rel)
    nf = np.maximum(n, 1).astype(np.float32)
    large = max_exact + (np.log(nf / np.float32(max_exact)) / np.float32(math.log(T5_MAX_DISTANCE / max_exact))
                         * np.float32(nb - max_exact)).astype(np.int32)
    large = np.minimum(large, nb - 1)
    return ret + np.where(n < max_exact, n, large)


def _table_lookup(table, idx):
    onehot = (jnp.asarray(idx, jnp.int32)[..., None] == jnp.arange(table.shape[0], dtype=jnp.int32)).astype(f32)
    return jnp.einsum("...n,nh->...h", onehot, table, precision=lax.Precision.HIGHEST)


def _dilated_bias(t5_table, dil):
    qi = np.arange(QBLK)[:, None]
    kj = np.arange(KBLK)[None, :]
    out = []
    for shift in (0, HALF, 2 * HALF):
        rel = kj - qi - shift
        band = np.abs(rel) <= HALF
        vals = _table_lookup(t5_table.astype(f32), _t5_buckets(rel * dil))
        vals = jnp.where(jnp.asarray(band)[:, :, None], vals, NEG_INF)
        out.append(vals.transpose(2, 0, 1))
    tab = jnp.stack(out, axis=1)
    tab = tab.reshape(HEAD_PAIRS, 2, 3, QBLK, KBLK).transpose(0, 2, 1, 3, 4)
    return tab.reshape(HEAD_PAIRS, 3, 2 * QBLK, KBLK)


def _neighbourhood_bias(rpb, rows):
    kh = min(NA_KH, rows)
    nvar = NA_KH if rows > kh else 1
    j = np.arange(GRID_W)[:, None]
    kc = np.arange(GRID_W)[None, :]
    ws = np.clip(j - NA_KW // 2, 0, GRID_W - NA_KW)
    inwin = (kc >= ws) & (kc < ws + NA_KW)
    dc = np.clip(kc - j + NA_KW - 1, 0, 2 * NA_KW - 2)
    ndr, ndc = rpb.shape[1], rpb.shape[2]
    cols = _table_lookup(rpb.astype(f32).reshape(N_HEADS * ndr, ndc).T, dc)
    cols = jnp.where(jnp.asarray(inwin)[:, :, None], cols, NEG_INF)
    cols = cols.reshape(GRID_W, GRID_W, N_HEADS, ndr).transpose(2, 3, 0, 1)
    out = []
    for var in range(nvar):
        lo = var + (NA_KH - kh)
        vals = cols[:, lo:lo + kh]
        out.append(vals.transpose(0, 2, 1, 3).reshape(N_HEADS, GRID_W, kh * GRID_W))
    tab = jnp.stack(out, axis=1)
    tab = tab.reshape(HEAD_PAIRS, 2, nvar, GRID_W, kh * GRID_W).transpose(0, 2, 1, 3, 4)
    return tab.reshape(HEAD_PAIRS, nvar, 2 * GRID_W, kh * GRID_W)


def _qkv_weight(w):
    hd = N_HEADS * HEAD_DIM
    scale = jnp.concatenate([jnp.full((hd,), HEAD_DIM ** -0.5, f32), jnp.ones((2 * hd,), f32)])
    return (w * scale[None, :]).astype(bf16)


def _router_weight(w_rg, b_rg, w_re, b_re):
    pad = LANES - N_GROUPS - N_EXPERTS
    wr = jnp.concatenate([w_rg, w_re, jnp.zeros((D_MODEL, pad), f32)], axis=1)
    br = jnp.concatenate([b_rg, b_re, jnp.zeros((pad,), f32)])[None, :]
    hi = wr.astype(bf16)
    lo = (wr - hi.astype(f32)).astype(bf16)
    return jnp.concatenate([hi, lo], axis=1), br


def _trunk(x3, t5_table, w_qkv_a, w_o_a, w_qkv_b, w_o_b, rpb_b, ln_g, ln_b,
           w_rg, b_rg, w_re, b_re, w1, w3, w2):
    B, S, D = x3.shape
    T = B * S
    n_slots = T + N_CLASSES * MOE_TM
    n_tiles = n_slots // MOE_TM
    hd = N_HEADS * HEAD_DIM
    x = x3.reshape(T, D)
    by_group = lambda w: w.astype(bf16).reshape(DEPTH, N_GROUPS, EXPERTS_PER_GROUP, *w.shape[2:])
    w1b, w3b, w2b = by_group(w1), by_group(w3), by_group(w2)
    bias_a = [_dilated_bias(t5_table, dil) for _, dil in DILATIONS]
    xs = jnp.zeros((n_slots, ROW_W), f32)
    for i in range(DEPTH):
        j = i // 2
        x3 = x.reshape(B, S, D)
        if i % 2 == 0:
            qkvs = [_qkv_proj(x3, _qkv_weight(w_qkv_a[j][:, g * 3 * hd:(g + 1) * 3 * hd]), dil)
                    for g, (_, dil) in enumerate(DILATIONS)]
            att = _attn_a(qkvs, bias_a)
            wo = w_o_a[j]
        else:
            qkv = _qkv_proj(x3, _qkv_weight(w_qkv_b[j]), 1)
            att = _attn_b(qkv, _neighbourhood_bias(rpb_b[j], S // GRID_W))
            wo = w_o_b[j]
        wr, br = _router_weight(w_rg[i], b_rg[i], w_re[i], b_re[i])
        rows = _oproj_ln_route(att, x, wo.astype(bf16), ln_g[i, 0][None, :], ln_b[i, 0][None, :], wr, br)
        cls = rows[:, D_MODEL + 2].astype(jnp.int32)
        pos, grp, la, lb, valid = _plan(cls, n_tiles)
        pos3 = pos.reshape(T // ROW_DMA_TM, 1, ROW_DMA_TM)
        xs = _dispatch(rows, pos3, xs)
        ys = _moe(xs, grp, la, lb, valid, w1b, w3b, w2b, i)
        x = _combine_ln(rows, ys, pos3, ln_g[i, 1][None, :], ln_b[i, 1][None, :])
    return x.reshape(B, S, D)


def kernel(x_prompt, x_sample, t5_table, w_qkv_a, w_o_a, w_qkv_b, w_o_b, rpb_b, ln_g, ln_b,
           w_rg, b_rg, w_re, b_re, w1, w3, w2):
    nb = x_prompt.shape[0]
    x = jnp.concatenate([x_prompt, x_sample], axis=0)
    y = _trunk(x, t5_table, w_qkv_a, w_o_a, w_qkv_b, w_o_b, rpb_b, ln_g, ln_b,
               w_rg, b_rg, w_re, b_re, w1, w3, w2)
    return (y[:nb], y[nb:])
```

```python
import functools
import math

import numpy as np
import jax
import jax.numpy as jnp
from jax import lax
from jax.experimental import pallas as pl
from jax.experimental.pallas import tpu as pltpu

D_MODEL = 1024
SEQ = 4096
DEPTH = 4
HEAD_DIM = 64
N_HEADS = 16
DILATIONS = ((128, 1), (512, 4), (2048, 16))
N_BUCKETS = 32
T5_MAX_DISTANCE = 1024
GRID_W = 64
NA_KH = 8
NA_KW = 16
N_GROUPS = 4
EXPERTS_PER_GROUP = 8
N_EXPERTS = N_GROUPS * EXPERTS_PER_GROUP
D_EXPERT = 512
ALPHA = (2 * DEPTH) ** 0.25
LN_EPS = 1e-5
NEG_INF = -1e30

LANES = 128
SUBLANES = 8
HEAD_PAIRS = N_HEADS * HEAD_DIM // LANES
QKV_BLOCKS = 3 * HEAD_PAIRS
VMEM_LIMIT = 56 * 1024 * 1024

QBLK = 128
KBLK = 256
HALF = 64
NA_ROWS_PER_ITER = 32
DIL_BLOCKS_PER_ITER = 16
PAIRS_PER_GROUP = EXPERTS_PER_GROUP * (EXPERTS_PER_GROUP - 1) // 2
N_CLASSES = N_GROUPS * PAIRS_PER_GROUP
ROW_W = D_MODEL + LANES
MOE_TM = 256
TOK_TM = 1024
ROW_DMA_TM = 1024
DMA_UNROLL = 8

f32 = jnp.float32
bf16 = jnp.bfloat16


def _params(n_grid):
    return pltpu.CompilerParams(dimension_semantics=("arbitrary",) * n_grid,
                                vmem_limit_bytes=VMEM_LIMIT)


def _select_source(refs, split, read):
    if len(refs) == 1:
        return read(refs[0])
    return jnp.where(pl.program_id(0) < split, read(refs[0]), read(refs[1]))


def _source_index(k, n_src, split):
    if n_src == 1:
        return lambda b: b
    return (lambda b: jnp.minimum(b, split - 1)) if k == 0 else (lambda b: jnp.maximum(b - split, 0))


def _qkv_kernel(*refs, dil, tm, cw, n_src, split):
    x_refs = refs[:n_src]
    w_ref, o_ref, acc_ref = refs[n_src:]
    nsub = cw // LANES
    rows = tm // dil
    rows_in = len(x_refs[0].shape) == 3
    if rows_in:
        xb = jnp.concatenate([_select_source(x_refs, split, lambda ref, r=r: ref[:, r, :]) for r in range(dil)],
                             axis=0).astype(bf16)
    else:
        xb = _select_source(x_refs, split, lambda ref: ref[...]).astype(bf16)
    for j in range(QKV_BLOCKS // nsub):
        res = jnp.dot(xb, w_ref[:, j * cw:(j + 1) * cw], preferred_element_type=f32)
        if dil == 1 or rows_in:
            for kk in range(nsub):
                for r in range(dil):
                    o_ref[j * nsub + kk, r, :, :] = res[r * rows:(r + 1) * rows,
                                                        kk * LANES:(kk + 1) * LANES].astype(bf16)
            continue
        for kk in range(nsub):
            acc_ref[kk] = res[:, kk * LANES:(kk + 1) * LANES]
        for kk in range(nsub):
            for r in range(dil):
                o_ref[j * nsub + kk, r, :, :] = acc_ref[kk, pl.ds(r, rows, stride=dil), :].astype(bf16)


def _qkv_proj(xs, w, dil):
    _, S, D = xs[0].shape
    n_src, split = len(xs), xs[0].shape[0]
    B = sum(x.shape[0] for x in xs)
    tm, cw = 1024, 512
    L = S // dil
    in_specs, args = [], []
    for k, x3 in enumerate(xs):
        src = _source_index(k, n_src, split)
        if dil % SUBLANES == 0:
            x3 = x3.reshape(x3.shape[0], L, dil, D)
            in_specs.append(pl.BlockSpec((None, tm // dil, dil, D), lambda b, i, src=src: (src(b), i, 0, 0)))
        else:
            in_specs.append(pl.BlockSpec((None, tm, D), lambda b, i, src=src: (src(b), i, 0)))
        args.append(x3)
    in_specs.append(pl.BlockSpec((D, QKV_BLOCKS * LANES), lambda b, i: (0, 0)))
    return pl.pallas_call(
        functools.partial(_qkv_kernel, dil=dil, tm=tm, cw=cw, n_src=n_src, split=split),
        grid=(B, S // tm),
        in_specs=in_specs,
        out_specs=pl.BlockSpec((None, QKV_BLOCKS, dil, tm // dil, LANES), lambda b, i: (b, 0, 0, i, 0)),
        out_shape=jax.ShapeDtypeStruct((B, QKV_BLOCKS, dil, L, LANES), bf16),
        scratch_shapes=[pltpu.VMEM((cw // LANES, tm, LANES), f32)],
        compiler_params=_params(2),
        name=f"qkv_proj_d{dil}",
    )(*args, w)


def _softmax_block(qst, kb, vb, bias):
    nq = qst.shape[0] // 2
    s = lax.dot_general(qst, kb, (((1,), (1,)), ((), ())), preferred_element_type=f32) + bias
    m = jnp.max(s, axis=1, keepdims=True)
    p = jnp.exp((s - m).astype(bf16))
    v1 = jnp.concatenate([vb, jnp.ones_like(vb)], axis=1)
    ol = jnp.dot(p, v1, preferred_element_type=f32)
    l = ol[:, LANES:]
    o2 = ol[:, :LANES] * (1.0 / l)
    lse = m + jnp.log(l)
    lane = lax.broadcasted_iota(jnp.int32, (nq, LANES), 1)
    o = jnp.where(lane < HEAD_DIM, o2[:nq], o2[nq:])
    ls = jnp.where(lane < HEAD_DIM, lse[:nq], lse[nq:])
    return o, ls


def _stack_heads(q2):
    lane = lax.broadcasted_iota(jnp.int32, q2.shape, 1)
    zero = jnp.zeros_like(q2)
    return jnp.concatenate([jnp.where(lane < HEAD_DIM, q2, zero),
                            jnp.where(lane >= HEAD_DIM, q2, zero)], axis=0)


def _attn_a_kernel(*refs):
    qkv = refs[:9]
    biases = refs[9:12]
    o_ref = refs[12]
    oscr, lscr = refs[13], refs[14]
    S = o_ref.shape[0]

    def block(it, g):
        dil = DILATIONS[g][1]
        q_ref, k_ref, v_ref = qkv[3 * g:3 * g + 3]
        L = S // dil
        nb = L // QBLK
        r = it // nb
        n = it % nb
        qs = pl.multiple_of(n * QBLK, QBLK)
        ks = pl.multiple_of(jnp.clip(qs - HALF, 0, L - KBLK), HALF)
        var = jnp.where(n == 0, 0, jnp.where(n == nb - 1, 2, 1))
        qst = _stack_heads(q_ref[r, pl.ds(qs, QBLK), :])
        o, ls = _softmax_block(qst, k_ref[r, pl.ds(ks, KBLK), :], v_ref[r, pl.ds(ks, KBLK), :], biases[g][var])
        if dil == 1:
            dst = pl.ds(qs, QBLK)
        else:
            dst = pl.ds(qs * dil + r, QBLK, stride=dil)
        oscr[g, dst, :] = o
        lscr[g, dst, :] = ls

    def body(it, carry):
        for u in range(DIL_BLOCKS_PER_ITER):
            for g in range(len(DILATIONS)):
                block(it * DIL_BLOCKS_PER_ITER + u, g)
        return carry

    lax.fori_loop(0, S // QBLK // DIL_BLOCKS_PER_ITER, body, 0)

    ch = 256

    def merge(c, carry):
        rows = pl.ds(pl.multiple_of(c * ch, ch), ch)
        l0, l1, l2 = lscr[0, rows, :], lscr[1, rows, :], lscr[2, rows, :]
        mx = jnp.maximum(jnp.maximum(l0, l1), l2)
        e0, e1, e2 = jnp.exp(l0 - mx), jnp.exp(l1 - mx), jnp.exp(l2 - mx)
        inv = 1.0 / (e0 + e1 + e2)
        o = (e0 * oscr[0, rows, :] + e1 * oscr[1, rows, :] + e2 * oscr[2, rows, :]) * inv
        o_ref[rows, :] = o.astype(bf16)
        return carry

    lax.fori_loop(0, S // ch, merge, 0)


def _attn_a(qkvs, biases):
    B = qkvs[0].shape[0]
    S = qkvs[0].shape[2] * qkvs[0].shape[3]
    in_specs, args = [], []
    for g, (_, dil) in enumerate(DILATIONS):
        L = S // dil
        for c in range(3):
            in_specs.append(pl.BlockSpec((None, None, dil, L, LANES),
                                         lambda b, hp, c=c: (b, c * HEAD_PAIRS + hp, 0, 0, 0)))
            args.append(qkvs[g])
    for g in range(3):
        in_specs.append(pl.BlockSpec((None, 3, 2 * QBLK, KBLK), lambda b, hp: (hp, 0, 0, 0)))
        args.append(biases[g])
    return pl.pallas_call(
        _attn_a_kernel,
        grid=(B, HEAD_PAIRS),
        in_specs=in_specs,
        out_specs=pl.BlockSpec((None, None, S, LANES), lambda b, hp: (b, hp, 0, 0)),
        out_shape=jax.ShapeDtypeStruct((B, HEAD_PAIRS, S, LANES), bf16),
        scratch_shapes=[pltpu.VMEM((3, S, LANES), f32), pltpu.VMEM((3, S, LANES), f32)],
        compiler_params=_params(2),
        name="dilated_attn",
    )(*args)


def _attn_b_kernel(q_ref, k_ref, v_ref, b_ref, o_ref):
    rows = o_ref.shape[0] // GRID_W
    kh = min(NA_KH, rows)
    nk = kh * GRID_W

    def one_row(i):
        rs = jnp.clip(i - kh // 2, 0, rows - kh)
        var = rs - i + NA_KH - 1 - (NA_KH - kh)
        qs = pl.multiple_of(i * GRID_W, GRID_W)
        ks = pl.multiple_of(rs * GRID_W, GRID_W)
        qst = _stack_heads(q_ref[pl.ds(qs, GRID_W), :])
        o, _ = _softmax_block(qst, k_ref[pl.ds(ks, nk), :], v_ref[pl.ds(ks, nk), :], b_ref[var])
        o_ref[pl.ds(qs, GRID_W), :] = o.astype(bf16)

    def body(it, carry):
        for u in range(NA_ROWS_PER_ITER):
            one_row(it * NA_ROWS_PER_ITER + u)
        return carry

    lax.fori_loop(0, rows // NA_ROWS_PER_ITER, body, 0)


def _attn_b(qkv, bias):
    B, _, _, S, _ = qkv.shape
    nvar, nk = bias.shape[1], bias.shape[3]
    in_specs = [pl.BlockSpec((None, None, None, S, LANES), lambda b, hp, c=c: (b, c * HEAD_PAIRS + hp, 0, 0, 0))
                for c in range(3)]
    in_specs.append(pl.BlockSpec((None, nvar, 2 * GRID_W, nk), lambda b, hp: (hp, 0, 0, 0)))
    return pl.pallas_call(
        _attn_b_kernel,
        grid=(B, HEAD_PAIRS),
        in_specs=in_specs,
        out_specs=pl.BlockSpec((None, None, S, LANES), lambda b, hp: (b, hp, 0, 0)),
        out_shape=jax.ShapeDtypeStruct((B, HEAD_PAIRS, S, LANES), bf16),
        compiler_params=_params(2),
        name="neighbourhood_attn",
    )(qkv, qkv, qkv, bias)


def _layer_norm(y, g, b):
    mu = jnp.mean(y, axis=1, keepdims=True)
    yc = y - mu
    var = jnp.mean(yc * yc, axis=1, keepdims=True)
    return yc * lax.rsqrt(var + LN_EPS) * g + b


def _route(logits):
    shape = logits.shape
    lane = lax.broadcasted_iota(jnp.int32, shape, 1)
    big = jnp.int32(4 * LANES)
    is_g = lane < N_GROUPS
    gl = jnp.where(is_g, logits, NEG_INF)
    mg = jnp.max(gl, axis=1, keepdims=True)
    gsel = jnp.min(jnp.where(gl == mg, lane, big), axis=1, keepdims=True)
    p_sel = 1.0 / jnp.sum(jnp.where(is_g, jnp.exp(gl - mg), 0.0), axis=1, keepdims=True)
    lo = N_GROUPS + gsel * EXPERTS_PER_GROUP
    el = jnp.where((lane >= lo) & (lane < lo + EXPERTS_PER_GROUP), logits, NEG_INF)
    v1 = jnp.max(el, axis=1, keepdims=True)
    i1 = jnp.min(jnp.where(el == v1, lane, big), axis=1, keepdims=True)
    el2 = jnp.where(lane == i1, NEG_INF, el)
    v2 = jnp.max(el2, axis=1, keepdims=True)
    i2 = jnp.min(jnp.where(el2 == v2, lane, big), axis=1, keepdims=True)
    t = jnp.exp(v2 - v1)
    g1 = p_sel / (1.0 + t)
    g2 = p_sel * t / (1.0 + t)
    first_lo = i1 < i2
    a = jnp.where(first_lo, i1, i2) - lo
    b = jnp.where(first_lo, i2, i1) - lo
    ga = jnp.where(first_lo, g1, g2)
    gb = jnp.where(first_lo, g2, g1)
    pair = (EXPERTS_PER_GROUP - 1) * a - ((a * (a - 1)) >> 1) + (b - a - 1)
    cls = (gsel * PAIRS_PER_GROUP + pair).astype(f32)
    return jnp.where(lane == 0, ga, jnp.where(lane == 1, gb, jnp.where(lane == 2, cls, 0.0)))


def _oproj_kernel(*refs, n_src, split):
    att_ref = refs[0]
    x_refs = refs[1:1 + n_src]
    wo_ref, g_ref, b_ref, wr_ref, br_ref, o_ref = refs[1 + n_src:]
    a = jnp.concatenate([att_ref[h] for h in range(HEAD_PAIRS)], axis=1)
    h = jnp.dot(a, wo_ref[...], preferred_element_type=f32)
    x = _select_source(x_refs, split, lambda ref: ref[...])
    xn = _layer_norm(ALPHA * x + h, g_ref[...], b_ref[...])
    xh = xn.astype(bf16)
    xl = (xn - xh.astype(f32)).astype(bf16)
    hw = jnp.dot(xh, wr_ref[...], preferred_element_type=f32)
    lw = jnp.dot(xl, wr_ref[:, :LANES], preferred_element_type=f32)
    logits = hw[:, :LANES] + hw[:, LANES:] + lw + br_ref[...]
    o_ref[:, :D_MODEL] = xn
    o_ref[:, D_MODEL:] = _route(logits)


def _oproj_ln_route(att, xs, wo, g, b, wr, br):
    B, _, S, _ = att.shape
    T = B * S
    tm = TOK_TM
    per = S // tm
    n_src, split = len(xs), xs[0].shape[0] // tm
    vec = lambda n: pl.BlockSpec((1, n), lambda i: (0, 0))
    x_specs = [pl.BlockSpec((tm, D_MODEL), lambda i, src=_source_index(k, n_src, split): (src(i), 0))
               for k in range(n_src)]
    return pl.pallas_call(
        functools.partial(_oproj_kernel, n_src=n_src, split=split),
        grid=(T // tm,),
        in_specs=[pl.BlockSpec((None, HEAD_PAIRS, tm, LANES), lambda i: (i // per, 0, i % per, 0))] + x_specs
                 + [pl.BlockSpec((D_MODEL, D_MODEL), lambda i: (0, 0)), vec(D_MODEL), vec(D_MODEL),
                    pl.BlockSpec((D_MODEL, 2 * LANES), lambda i: (0, 0)), vec(LANES)],
        out_specs=pl.BlockSpec((tm, ROW_W), lambda i: (i, 0)),
        out_shape=jax.ShapeDtypeStruct((T, ROW_W), f32),
        compiler_params=_params(1),
        name="oproj_ln_route",
    )(att, *xs, wo, g, b, wr, br)


def _for_rows(n, fn):
    def body(c, carry):
        base = pl.multiple_of(c * DMA_UNROLL, DMA_UNROLL)
        for u in range(DMA_UNROLL):
            fn(base, u)
        return carry

    lax.fori_loop(0, n // DMA_UNROLL, body, 0)


def _tile_row(ref, base, u):
    return ref.at[pl.ds(base, DMA_UNROLL), :].at[pl.ds(u, 1), :]


def _dispatch_kernel(pos_ref, rows_ref, init_ref, xs_ref, sem):
    del init_ref

    def copy(base, u):
        dst = xs_ref.at[pl.ds(pos_ref[0, 0, base + u], 1), :]
        return pltpu.make_async_copy(_tile_row(rows_ref, base, u), dst, sem)

    n = rows_ref.shape[0]
    _for_rows(n, lambda base, u: copy(base, u).start())
    _for_rows(n, lambda base, u: copy(base, u).wait())


def _dispatch(rows, pos3, slots):
    T = rows.shape[0]
    tm = pos3.shape[2]
    return pl.pallas_call(
        _dispatch_kernel,
        grid=(T // tm,),
        in_specs=[pl.BlockSpec((1, 1, tm), lambda i: (i, 0, 0), memory_space=pltpu.SMEM),
                  pl.BlockSpec((tm, ROW_W), lambda i: (i, 0)),
                  pl.BlockSpec(memory_space=pl.ANY)],
        out_specs=pl.BlockSpec(memory_space=pl.ANY),
        out_shape=jax.ShapeDtypeStruct(slots.shape, f32),
        scratch_shapes=[pltpu.SemaphoreType.DMA(())],
        input_output_aliases={2: 0},
        compiler_params=_params(1),
        name="moe_dispatch",
    )(pos3, rows, slots)


def _swiglu(xt, w1, w3, w2):
    h1 = jnp.dot(xt, w1, preferred_element_type=f32)
    h3 = jnp.dot(xt, w3, preferred_element_type=f32)
    h = h1 * (1.0 / (1.0 + jnp.exp(-h1))) * h3
    return jnp.dot(h.astype(bf16), w2, preferred_element_type=f32)


def _moe_kernel(grp_ref, la_ref, lb_ref, valid_ref, xs_ref, w1_ref, w3_ref, w2_ref, ys_ref):
    del grp_ref
    i = pl.program_id(0)

    @pl.when(valid_ref[i] != 0)
    def _():
        a, b = la_ref[i], lb_ref[i]
        xt = xs_ref[:, :D_MODEL].astype(bf16)
        ga = xs_ref[:, D_MODEL:D_MODEL + 1]
        gb = xs_ref[:, D_MODEL + 1:D_MODEL + 2]
        ys_ref[...] = (ga * _swiglu(xt, w1_ref[a], w3_ref[a], w2_ref[a])
                       + gb * _swiglu(xt, w1_ref[b], w3_ref[b], w2_ref[b]))

    @pl.when(valid_ref[i] == 0)
    def _():
        ys_ref[...] = jnp.zeros_like(ys_ref)


def _moe(xs, tile_grp, tile_a, tile_b, tile_valid, w1, w3, w2, layer):
    n_slots = xs.shape[0]
    tm = MOE_TM
    group_w = lambda shape: pl.BlockSpec((None, None, EXPERTS_PER_GROUP) + shape,
                                         lambda i, g, a, b, va: (layer, g[i], 0, 0, 0),
                                         pipeline_mode=pl.Buffered(1))
    up, down = (D_MODEL, D_EXPERT), (D_EXPERT, D_MODEL)
    grid_spec = pltpu.PrefetchScalarGridSpec(
        num_scalar_prefetch=4,
        grid=(n_slots // tm,),
        in_specs=[pl.BlockSpec((tm, ROW_W), lambda i, g, a, b, va: (i, 0)),
                  group_w(up), group_w(up), group_w(down)],
        out_specs=pl.BlockSpec((tm, D_MODEL), lambda i, g, a, b, va: (i, 0)),
    )
    return pl.pallas_call(
        _moe_kernel,
        grid_spec=grid_spec,
        out_shape=jax.ShapeDtypeStruct((n_slots, D_MODEL), f32),
        compiler_params=_params(1),
        name="moe_experts",
    )(tile_grp, tile_a, tile_b, tile_valid, xs, w1, w3, w2)


def _combine_kernel(pos_ref, pos_next_ref, x_ref, ys_ref, g_ref, b_ref, o_ref, buf, sem):
    i = pl.program_id(0)
    n = x_ref.shape[0]
    slot = i % 2

    def copy(p_ref, s):
        def make(base, u):
            src = ys_ref.at[pl.ds(p_ref[0, 0, base + u], 1), :]
            return pltpu.make_async_copy(src, _tile_row(buf.at[s], base, u), sem.at[s])
        return make

    @pl.when(i == 0)
    def _():
        _for_rows(n, lambda base, u: copy(pos_ref, slot)(base, u).start())

    @pl.when(i + 1 < pl.num_programs(0))
    def _():
        _for_rows(n, lambda base, u: copy(pos_next_ref, 1 - slot)(base, u).start())

    _for_rows(n, lambda base, u: copy(pos_ref, slot)(base, u).wait())
    o_ref[...] = _layer_norm(ALPHA * x_ref[...] + buf[slot], g_ref[...], b_ref[...])


def _combine_ln(rows, ys, pos3, g, b, t0, nt):
    tm = pos3.shape[2]
    vec = pl.BlockSpec((1, D_MODEL), lambda i: (0, 0))
    return pl.pallas_call(
        _combine_kernel,
        grid=(nt,),
        in_specs=[pl.BlockSpec((1, 1, tm), lambda i: (i + t0, 0, 0), memory_space=pltpu.SMEM),
                  pl.BlockSpec((1, 1, tm), lambda i: (jnp.minimum(i + 1, nt - 1) + t0, 0, 0),
                               memory_space=pltpu.SMEM),
                  pl.BlockSpec((tm, D_MODEL), lambda i: (i + t0, 0)),
                  pl.BlockSpec(memory_space=pl.ANY), vec, vec],
        out_specs=pl.BlockSpec((tm, D_MODEL), lambda i: (i, 0)),
        out_shape=jax.ShapeDtypeStruct((nt * tm, D_MODEL), f32),
        scratch_shapes=[pltpu.VMEM((2, tm, D_MODEL), f32), pltpu.SemaphoreType.DMA((2,))],
        compiler_params=_params(1),
        name="moe_combine_ln",
    )(pos3, pos3, rows, ys, g, b)


def _plan(cls, n_tiles):
    onehot = (cls[:, None] == jnp.arange(N_CLASSES, dtype=jnp.int32)[None, :]).astype(jnp.int32)
    csum = jnp.cumsum(onehot, axis=0)
    rank = jnp.sum(csum * onehot, axis=1) - 1
    counts = csum[-1]
    tiles = (counts + MOE_TM - 1) // MOE_TM
    tile_end = jnp.cumsum(tiles)
    tile_start = tile_end - tiles
    pos = jnp.sum(onehot * tile_start[None, :], axis=1) * MOE_TM + rank
    tile_id = jnp.arange(n_tiles, dtype=jnp.int32)
    used = tile_end[-1]
    tcls = jnp.searchsorted(tile_end, jnp.minimum(tile_id, used - 1), side="right").astype(jnp.int32)
    grp = tcls // PAIRS_PER_GROUP
    pair = tcls % PAIRS_PER_GROUP
    pa, pb = np.triu_indices(EXPERTS_PER_GROUP, 1)
    la = jnp.asarray(pa, jnp.int32)[pair]
    lb = jnp.asarray(pb, jnp.int32)[pair]
    valid = (tile_id < used).astype(jnp.int32)
    return pos.astype(jnp.int32), grp, la, lb, valid


def _t5_buckets(rel):
    nb = N_BUCKETS // 2
    max_exact = nb // 2
    ret = np.where(rel > 0, nb, 0)
    n = np.abs(rel)
    nf = np.maximum(n, 1).astype(np.float32)
    large = max_exact + (np.log(nf / np.float32(max_exact)) / np.float32(math.log(T5_MAX_DISTANCE / max_exact))
                         * np.float32(nb - max_exact)).astype(np.int32)
    large = np.minimum(large, nb - 1)
    return ret + np.where(n < max_exact, n, large)


def _table_lookup(table, idx):
    onehot = (jnp.asarray(idx, jnp.int32)[..., None] == jnp.arange(table.shape[0], dtype=jnp.int32)).astype(f32)
    return jnp.einsum("...n,nh->...h", onehot, table, precision=lax.Precision.HIGHEST)


def _dilated_bias(t5_table, dil):
    qi = np.arange(QBLK)[:, None]
    kj = np.arange(KBLK)[None, :]
    out = []
    for shift in (0, HALF, 2 * HALF):
        rel = kj - qi - shift
        band = np.abs(rel) <= HALF
        vals = _table_lookup(t5_table.astype(f32), _t5_buckets(rel * dil))
        vals = jnp.where(jnp.asarray(band)[:, :, None], vals, NEG_INF)
        out.append(vals.transpose(2, 0, 1))
    tab = jnp.stack(out, axis=1)
    tab = tab.reshape(HEAD_PAIRS, 2, 3, QBLK, KBLK).transpose(0, 2, 1, 3, 4)
    return tab.reshape(HEAD_PAIRS, 3, 2 * QBLK, KBLK)


def _neighbourhood_bias(rpb, rows):
    kh = min(NA_KH, rows)
    nvar = NA_KH if rows > kh else 1
    j = np.arange(GRID_W)[:, None]
    kc = np.arange(GRID_W)[None, :]
    ws = np.clip(j - NA_KW // 2, 0, GRID_W - NA_KW)
    inwin = (kc >= ws) & (kc < ws + NA_KW)
    dc = np.clip(kc - j + NA_KW - 1, 0, 2 * NA_KW - 2)
    ndr, ndc = rpb.shape[1], rpb.shape[2]
    cols = _table_lookup(rpb.astype(f32).reshape(N_HEADS * ndr, ndc).T, dc)
    cols = jnp.where(jnp.asarray(inwin)[:, :, None], cols, NEG_INF)
    cols = cols.reshape(GRID_W, GRID_W, N_HEADS, ndr).transpose(2, 3, 0, 1)
    out = []
    for var in range(nvar):
        lo = var + (NA_KH - kh)
        vals = cols[:, lo:lo + kh]
        out.append(vals.transpose(0, 2, 1, 3).reshape(N_HEADS, GRID_W, kh * GRID_W))
    tab = jnp.stack(out, axis=1)
    tab = tab.reshape(HEAD_PAIRS, 2, nvar, GRID_W, kh * GRID_W).transpose(0, 2, 1, 3, 4)
    return tab.reshape(HEAD_PAIRS, nvar, 2 * GRID_W, kh * GRID_W)


def _qkv_weight(w):
    hd = N_HEADS * HEAD_DIM
    scale = jnp.concatenate([jnp.full((hd,), HEAD_DIM ** -0.5, f32), jnp.ones((2 * hd,), f32)])
    return (w * scale[None, :]).astype(bf16)


def _router_weight(w_rg, b_rg, w_re, b_re):
    pad = LANES - N_GROUPS - N_EXPERTS
    wr = jnp.concatenate([w_rg, w_re, jnp.zeros((D_MODEL, pad), f32)], axis=1)
    br = jnp.concatenate([b_rg, b_re, jnp.zeros((pad,), f32)])[None, :]
    hi = wr.astype(bf16)
    lo = (wr - hi.astype(f32)).astype(bf16)
    return jnp.concatenate([hi, lo], axis=1), br


def _trunk(x_parts, t5_table, w_qkv_a, w_o_a, w_qkv_b, w_o_b, rpb_b, ln_g, ln_b,
           w_rg, b_rg, w_re, b_re, w1, w3, w2):
    _, S, D = x_parts[0].shape
    batches = [x.shape[0] for x in x_parts]
    B = sum(batches)
    T = B * S
    n_slots = T + N_CLASSES * MOE_TM
    n_tiles = n_slots // MOE_TM
    hd = N_HEADS * HEAD_DIM
    by_group = lambda w: w.astype(bf16).reshape(DEPTH, N_GROUPS, EXPERTS_PER_GROUP, *w.shape[2:])
    w1b, w3b, w2b = by_group(w1), by_group(w3), by_group(w2)
    bias_a = [_dilated_bias(t5_table, dil) for _, dil in DILATIONS]
    xs = jnp.zeros((n_slots, ROW_W), f32)
    x3s = list(x_parts)
    for i in range(DEPTH):
        j = i // 2
        if i % 2 == 0:
            qkvs = [_qkv_proj(x3s, _qkv_weight(w_qkv_a[j][:, g * 3 * hd:(g + 1) * 3 * hd]), dil)
                    for g, (_, dil) in enumerate(DILATIONS)]
            att = _attn_a(qkvs, bias_a)
            wo = w_o_a[j]
        else:
            qkv = _qkv_proj(x3s, _qkv_weight(w_qkv_b[j]), 1)
            att = _attn_b(qkv, _neighbourhood_bias(rpb_b[j], S // GRID_W))
            wo = w_o_b[j]
        wr, br = _router_weight(w_rg[i], b_rg[i], w_re[i], b_re[i])
        rows = _oproj_ln_route(att, [x.reshape(-1, D) for x in x3s], wo.astype(bf16),
                               ln_g[i, 0][None, :], ln_b[i, 0][None, :], wr, br)
        cls = rows[:, D_MODEL + 2].astype(jnp.int32)
        pos, grp, la, lb, valid = _plan(cls, n_tiles)
        pos3 = pos.reshape(T // ROW_DMA_TM, 1, ROW_DMA_TM)
        xs = _dispatch(rows, pos3, xs)
        ys = _moe(xs, grp, la, lb, valid, w1b, w3b, w2b, i)
        out_batches = batches if i == DEPTH - 1 else [B]
        x3s, t0 = [], 0
        for nb in out_batches:
            nt = nb * S // ROW_DMA_TM
            y = _combine_ln(rows, ys, pos3, ln_g[i, 1][None, :], ln_b[i, 1][None, :], t0, nt)
            x3s.append(y.reshape(nb, S, D))
            t0 += nt
    return x3s


def kernel(x_prompt, x_sample, t5_table, w_qkv_a, w_o_a, w_qkv_b, w_o_b, rpb_b, ln_g, ln_b,
           w_rg, b_rg, w_re, b_re, w1, w3, w2):
    y_prompt, y_sample = _trunk([x_prompt, x_sample], t5_table, w_qkv_a, w_o_a, w_qkv_b, w_o_b, rpb_b,
                                ln_g, ln_b, w_rg, b_rg, w_re, b_re, w1, w3, w2)
    return (y_prompt, y_sample)
```

```python
import functools
import math

import numpy as np
import jax
import jax.numpy as jnp
from jax import lax
from jax.experimental import pallas as pl
from jax.experimental.pallas import tpu as pltpu

D_MODEL = 1024
SEQ = 4096
DEPTH = 4
HEAD_DIM = 64
N_HEADS = 16
DILATIONS = ((128, 1), (512, 4), (2048, 16))
N_BUCKETS = 32
T5_MAX_DISTANCE = 1024
GRID_W = 64
NA_KH = 8
NA_KW = 16
N_GROUPS = 4
EXPERTS_PER_GROUP = 8
N_EXPERTS = N_GROUPS * EXPERTS_PER_GROUP
D_EXPERT = 512
ALPHA = (2 * DEPTH) ** 0.25
LN_EPS = 1e-5
NEG_INF = -1e30

LANES = 128
SUBLANES = 8
HEAD_PAIRS = N_HEADS * HEAD_DIM // LANES
QKV_BLOCKS = 3 * HEAD_PAIRS
VMEM_LIMIT = 56 * 1024 * 1024

QBLK = 128
KBLK = 256
HALF = 64
NA_ROWS_PER_ITER = 32
DIL_BLOCKS_PER_ITER = 16
PAIRS_PER_GROUP = EXPERTS_PER_GROUP * (EXPERTS_PER_GROUP - 1) // 2
N_CLASSES = N_GROUPS * PAIRS_PER_GROUP
ROW_W = D_MODEL + LANES
MOE_TM = 256
PLAN_BLOCK = 512
TOK_TM = 1024
ROW_DMA_TM = 1024
DMA_UNROLL = 8

f32 = jnp.float32
bf16 = jnp.bfloat16


def _params(n_grid):
    return pltpu.CompilerParams(dimension_semantics=("arbitrary",) * n_grid,
                                vmem_limit_bytes=VMEM_LIMIT)


def _select_source(refs, split, read):
    if len(refs) == 1:
        return read(refs[0])
    return jnp.where(pl.program_id(0) < split, read(refs[0]), read(refs[1]))


def _source_index(k, n_src, split):
    if n_src == 1:
        return lambda b: b
    return (lambda b: jnp.minimum(b, split - 1)) if k == 0 else (lambda b: jnp.maximum(b - split, 0))


def _qkv_kernel(*refs, dil, tm, cw, n_src, split):
    x_refs = refs[:n_src]
    w_ref, o_ref, acc_ref = refs[n_src:]
    nsub = cw // LANES
    rows = tm // dil
    rows_in = len(x_refs[0].shape) == 3
    if rows_in:
        xb = jnp.concatenate([_select_source(x_refs, split, lambda ref, r=r: ref[:, r, :]) for r in range(dil)],
                             axis=0).astype(bf16)
    else:
        xb = _select_source(x_refs, split, lambda ref: ref[...]).astype(bf16)
    for j in range(QKV_BLOCKS // nsub):
        res = jnp.dot(xb, w_ref[:, j * cw:(j + 1) * cw], preferred_element_type=f32)
        if dil == 1 or rows_in:
            for kk in range(nsub):
                for r in range(dil):
                    o_ref[j * nsub + kk, r, :, :] = res[r * rows:(r + 1) * rows,
                                                        kk * LANES:(kk + 1) * LANES].astype(bf16)
            continue
        for kk in range(nsub):
            acc_ref[kk] = res[:, kk * LANES:(kk + 1) * LANES]
        for kk in range(nsub):
            for r in range(dil):
                o_ref[j * nsub + kk, r, :, :] = acc_ref[kk, pl.ds(r, rows, stride=dil), :].astype(bf16)


def _qkv_proj(xs, w, dil):
    _, S, D = xs[0].shape
    n_src, split = len(xs), xs[0].shape[0]
    B = sum(x.shape[0] for x in xs)
    tm, cw = 1024, 512
    L = S // dil
    in_specs, args = [], []
    for k, x3 in enumerate(xs):
        src = _source_index(k, n_src, split)
        if dil % SUBLANES == 0:
            x3 = x3.reshape(x3.shape[0], L, dil, D)
            in_specs.append(pl.BlockSpec((None, tm // dil, dil, D), lambda b, i, src=src: (src(b), i, 0, 0)))
        else:
            in_specs.append(pl.BlockSpec((None, tm, D), lambda b, i, src=src: (src(b), i, 0)))
        args.append(x3)
    in_specs.append(pl.BlockSpec((D, QKV_BLOCKS * LANES), lambda b, i: (0, 0)))
    return pl.pallas_call(
        functools.partial(_qkv_kernel, dil=dil, tm=tm, cw=cw, n_src=n_src, split=split),
        grid=(B, S // tm),
        in_specs=in_specs,
        out_specs=pl.BlockSpec((None, QKV_BLOCKS, dil, tm // dil, LANES), lambda b, i: (b, 0, 0, i, 0)),
        out_shape=jax.ShapeDtypeStruct((B, QKV_BLOCKS, dil, L, LANES), bf16),
        scratch_shapes=[pltpu.VMEM((cw // LANES, tm, LANES), f32)],
        compiler_params=_params(2),
        name=f"qkv_proj_d{dil}",
    )(*args, w)


def _softmax_block(qst, kb, vb, bias):
    nq = qst.shape[0] // 2
    s = lax.dot_general(qst, kb, (((1,), (1,)), ((), ())), preferred_element_type=f32) + bias
    m = jnp.max(s, axis=1, keepdims=True)
    p = jnp.exp((s - m).astype(bf16))
    v1 = jnp.concatenate([vb, jnp.ones_like(vb)], axis=1)
    ol = jnp.dot(p, v1, preferred_element_type=f32)
    l = ol[:, LANES:]
    o2 = ol[:, :LANES] * (1.0 / l)
    lse = m + jnp.log(l)
    lane = lax.broadcasted_iota(jnp.int32, (nq, LANES), 1)
    o = jnp.where(lane < HEAD_DIM, o2[:nq], o2[nq:])
    ls = jnp.where(lane < HEAD_DIM, lse[:nq], lse[nq:])
    return o, ls


def _stack_heads(q2):
    lane = lax.broadcasted_iota(jnp.int32, q2.shape, 1)
    zero = jnp.zeros_like(q2)
    return jnp.concatenate([jnp.where(lane < HEAD_DIM, q2, zero),
                            jnp.where(lane >= HEAD_DIM, q2, zero)], axis=0)


def _attn_a_kernel(*refs):
    qkv = refs[:9]
    biases = refs[9:12]
    o_ref = refs[12]
    oscr, lscr = refs[13], refs[14]
    S = o_ref.shape[0]

    def block(it, g):
        dil = DILATIONS[g][1]
        q_ref, k_ref, v_ref = qkv[3 * g:3 * g + 3]
        L = S // dil
        nb = L // QBLK
        r = it // nb
        n = it % nb
        qs = pl.multiple_of(n * QBLK, QBLK)
        ks = pl.multiple_of(jnp.clip(qs - HALF, 0, L - KBLK), HALF)
        var = jnp.where(n == 0, 0, jnp.where(n == nb - 1, 2, 1))
        qst = _stack_heads(q_ref[r, pl.ds(qs, QBLK), :])
        o, ls = _softmax_block(qst, k_ref[r, pl.ds(ks, KBLK), :], v_ref[r, pl.ds(ks, KBLK), :], biases[g][var])
        if dil == 1:
            dst = pl.ds(qs, QBLK)
        else:
            dst = pl.ds(qs * dil + r, QBLK, stride=dil)
        oscr[g, dst, :] = o
        lscr[g, dst, :] = ls

    def body(it, carry):
        for u in range(DIL_BLOCKS_PER_ITER):
            for g in range(len(DILATIONS)):
                block(it * DIL_BLOCKS_PER_ITER + u, g)
        return carry

    lax.fori_loop(0, S // QBLK // DIL_BLOCKS_PER_ITER, body, 0)

    ch = 256

    def merge(c, carry):
        rows = pl.ds(pl.multiple_of(c * ch, ch), ch)
        l0, l1, l2 = lscr[0, rows, :], lscr[1, rows, :], lscr[2, rows, :]
        mx = jnp.maximum(jnp.maximum(l0, l1), l2)
        e0, e1, e2 = jnp.exp(l0 - mx), jnp.exp(l1 - mx), jnp.exp(l2 - mx)
        inv = 1.0 / (e0 + e1 + e2)
        o = (e0 * oscr[0, rows, :] + e1 * oscr[1, rows, :] + e2 * oscr[2, rows, :]) * inv
        o_ref[rows, :] = o.astype(bf16)
        return carry

    lax.fori_loop(0, S // ch, merge, 0)


def _attn_a(qkvs, biases):
    B = qkvs[0].shape[0]
    S = qkvs[0].shape[2] * qkvs[0].shape[3]
    in_specs, args = [], []
    for g, (_, dil) in enumerate(DILATIONS):
        L = S // dil
        for c in range(3):
            in_specs.append(pl.BlockSpec((None, None, dil, L, LANES),
                                         lambda b, hp, c=c: (b, c * HEAD_PAIRS + hp, 0, 0, 0)))
            args.append(qkvs[g])
    for g in range(3):
        in_specs.append(pl.BlockSpec((None, 3, 2 * QBLK, KBLK), lambda b, hp: (hp, 0, 0, 0)))
        args.append(biases[g])
    return pl.pallas_call(
        _attn_a_kernel,
        grid=(B, HEAD_PAIRS),
        in_specs=in_specs,
        out_specs=pl.BlockSpec((None, None, S, LANES), lambda b, hp: (b, hp, 0, 0)),
        out_shape=jax.ShapeDtypeStruct((B, HEAD_PAIRS, S, LANES), bf16),
        scratch_shapes=[pltpu.VMEM((3, S, LANES), f32), pltpu.VMEM((3, S, LANES), f32)],
        compiler_params=_params(2),
        name="dilated_attn",
    )(*args)


def _attn_b_kernel(q_ref, k_ref, v_ref, b_ref, o_ref):
    rows = o_ref.shape[0] // GRID_W
    kh = min(NA_KH, rows)
    nk = kh * GRID_W

    def one_row(i):
        rs = jnp.clip(i - kh // 2, 0, rows - kh)
        var = rs - i + NA_KH - 1 - (NA_KH - kh)
        qs = pl.multiple_of(i * GRID_W, GRID_W)
        ks = pl.multiple_of(rs * GRID_W, GRID_W)
        qst = _stack_heads(q_ref[pl.ds(qs, GRID_W), :])
        o, _ = _softmax_block(qst, k_ref[pl.ds(ks, nk), :], v_ref[pl.ds(ks, nk), :], b_ref[var])
        o_ref[pl.ds(qs, GRID_W), :] = o.astype(bf16)

    def body(it, carry):
        for u in range(NA_ROWS_PER_ITER):
            one_row(it * NA_ROWS_PER_ITER + u)
        return carry

    lax.fori_loop(0, rows // NA_ROWS_PER_ITER, body, 0)


def _attn_b(qkv, bias):
    B, _, _, S, _ = qkv.shape
    nvar, nk = bias.shape[1], bias.shape[3]
    in_specs = [pl.BlockSpec((None, None, None, S, LANES), lambda b, hp, c=c: (b, c * HEAD_PAIRS + hp, 0, 0, 0))
                for c in range(3)]
    in_specs.append(pl.BlockSpec((None, nvar, 2 * GRID_W, nk), lambda b, hp: (hp, 0, 0, 0)))
    return pl.pallas_call(
        _attn_b_kernel,
        grid=(B, HEAD_PAIRS),
        in_specs=in_specs,
        out_specs=pl.BlockSpec((None, None, S, LANES), lambda b, hp: (b, hp, 0, 0)),
        out_shape=jax.ShapeDtypeStruct((B, HEAD_PAIRS, S, LANES), bf16),
        compiler_params=_params(2),
        name="neighbourhood_attn",
    )(qkv, qkv, qkv, bias)


def _layer_norm(y, g, b):
    mu = jnp.mean(y, axis=1, keepdims=True)
    yc = y - mu
    var = jnp.mean(yc * yc, axis=1, keepdims=True)
    return yc * lax.rsqrt(var + LN_EPS) * g + b


def _route(logits):
    shape = logits.shape
    lane = lax.broadcasted_iota(jnp.int32, shape, 1)
    big = jnp.int32(4 * LANES)
    is_g = lane < N_GROUPS
    gl = jnp.where(is_g, logits, NEG_INF)
    mg = jnp.max(gl, axis=1, keepdims=True)
    gsel = jnp.min(jnp.where(gl == mg, lane, big), axis=1, keepdims=True)
    p_sel = 1.0 / jnp.sum(jnp.where(is_g, jnp.exp(gl - mg), 0.0), axis=1, keepdims=True)
    lo = N_GROUPS + gsel * EXPERTS_PER_GROUP
    el = jnp.where((lane >= lo) & (lane < lo + EXPERTS_PER_GROUP), logits, NEG_INF)
    v1 = jnp.max(el, axis=1, keepdims=True)
    i1 = jnp.min(jnp.where(el == v1, lane, big), axis=1, keepdims=True)
    el2 = jnp.where(lane == i1, NEG_INF, el)
    v2 = jnp.max(el2, axis=1, keepdims=True)
    i2 = jnp.min(jnp.where(el2 == v2, lane, big), axis=1, keepdims=True)
    t = jnp.exp(v2 - v1)
    g1 = p_sel / (1.0 + t)
    g2 = p_sel * t / (1.0 + t)
    first_lo = i1 < i2
    a = jnp.where(first_lo, i1, i2) - lo
    b = jnp.where(first_lo, i2, i1) - lo
    ga = jnp.where(first_lo, g1, g2)
    gb = jnp.where(first_lo, g2, g1)
    pair = (EXPERTS_PER_GROUP - 1) * a - ((a * (a - 1)) >> 1) + (b - a - 1)
    cls = (gsel * PAIRS_PER_GROUP + pair).astype(f32)
    return jnp.where(lane == 0, ga, jnp.where(lane == 1, gb, jnp.where(lane == 2, cls, 0.0)))


def _oproj_kernel(*refs, n_src, split):
    att_ref = refs[0]
    x_refs = refs[1:1 + n_src]
    wo_ref, g_ref, b_ref, wr_ref, br_ref, o_ref = refs[1 + n_src:]
    a = jnp.concatenate([att_ref[h] for h in range(HEAD_PAIRS)], axis=1)
    h = jnp.dot(a, wo_ref[...], preferred_element_type=f32)
    x = _select_source(x_refs, split, lambda ref: ref[...])
    xn = _layer_norm(ALPHA * x + h, g_ref[...], b_ref[...])
    xh = xn.astype(bf16)
    xl = (xn - xh.astype(f32)).astype(bf16)
    hw = jnp.dot(xh, wr_ref[...], preferred_element_type=f32)
    lw = jnp.dot(xl, wr_ref[:, :LANES], preferred_element_type=f32)
    logits = hw[:, :LANES] + hw[:, LANES:] + lw + br_ref[...]
    o_ref[:, :D_MODEL] = xn
    o_ref[:, D_MODEL:] = _route(logits)


def _oproj_ln_route(att, xs, wo, g, b, wr, br):
    B, _, S, _ = att.shape
    T = B * S
    tm = TOK_TM
    per = S // tm
    n_src, split = len(xs), xs[0].shape[0] // tm
    vec = lambda n: pl.BlockSpec((1, n), lambda i: (0, 0))
    x_specs = [pl.BlockSpec((tm, D_MODEL), lambda i, src=_source_index(k, n_src, split): (src(i), 0))
               for k in range(n_src)]
    return pl.pallas_call(
        functools.partial(_oproj_kernel, n_src=n_src, split=split),
        grid=(T // tm,),
        in_specs=[pl.BlockSpec((None, HEAD_PAIRS, tm, LANES), lambda i: (i // per, 0, i % per, 0))] + x_specs
                 + [pl.BlockSpec((D_MODEL, D_MODEL), lambda i: (0, 0)), vec(D_MODEL), vec(D_MODEL),
                    pl.BlockSpec((D_MODEL, 2 * LANES), lambda i: (0, 0)), vec(LANES)],
        out_specs=pl.BlockSpec((tm, ROW_W), lambda i: (i, 0)),
        out_shape=jax.ShapeDtypeStruct((T, ROW_W), f32),
        compiler_params=_params(1),
        name="oproj_ln_route",
    )(att, *xs, wo, g, b, wr, br)


def _for_rows(n, fn):
    def body(c, carry):
        base = pl.multiple_of(c * DMA_UNROLL, DMA_UNROLL)
        for u in range(DMA_UNROLL):
            fn(base, u)
        return carry

    lax.fori_loop(0, n // DMA_UNROLL, body, 0)


def _tile_row(ref, base, u):
    return ref.at[pl.ds(base, DMA_UNROLL), :].at[pl.ds(u, 1), :]


def _dispatch_kernel(pos_ref, rows_ref, init_ref, xs_ref, sem):
    del init_ref

    def copy(base, u):
        dst = xs_ref.at[pl.ds(pos_ref[0, 0, base + u], 1), :]
        return pltpu.make_async_copy(_tile_row(rows_ref, base, u), dst, sem)

    n = rows_ref.shape[0]
    _for_rows(n, lambda base, u: copy(base, u).start())
    _for_rows(n, lambda base, u: copy(base, u).wait())


def _dispatch(rows, pos3, slots):
    T = rows.shape[0]
    tm = pos3.shape[2]
    return pl.pallas_call(
        _dispatch_kernel,
        grid=(T // tm,),
        in_specs=[pl.BlockSpec((1, 1, tm), lambda i: (i, 0, 0), memory_space=pltpu.SMEM),
                  pl.BlockSpec((tm, ROW_W), lambda i: (i, 0)),
                  pl.BlockSpec(memory_space=pl.ANY)],
        out_specs=pl.BlockSpec(memory_space=pl.ANY),
        out_shape=jax.ShapeDtypeStruct(slots.shape, f32),
        scratch_shapes=[pltpu.SemaphoreType.DMA(())],
        input_output_aliases={2: 0},
        compiler_params=_params(1),
        name="moe_dispatch",
    )(pos3, rows, slots)


def _swiglu(xt, w1, w3, w2):
    h1 = jnp.dot(xt, w1, preferred_element_type=f32)
    h3 = jnp.dot(xt, w3, preferred_element_type=f32)
    h = h1 * (1.0 / (1.0 + jnp.exp(-h1))) * h3
    return jnp.dot(h.astype(bf16), w2, preferred_element_type=f32)


def _moe_kernel(grp_ref, la_ref, lb_ref, nrows_ref, xs_ref, w1_ref, w3_ref, w2_ref, ys_ref):
    del grp_ref
    i = pl.program_id(0)
    tm = xs_ref.shape[0]
    half = tm // 2

    def experts(rows):
        a, b = la_ref[i], lb_ref[i]
        xt = xs_ref[rows, :D_MODEL].astype(bf16)
        ga = xs_ref[rows, D_MODEL:D_MODEL + 1]
        gb = xs_ref[rows, D_MODEL + 1:D_MODEL + 2]
        ys_ref[rows, :] = (ga * _swiglu(xt, w1_ref[a], w3_ref[a], w2_ref[a])
                           + gb * _swiglu(xt, w1_ref[b], w3_ref[b], w2_ref[b]))

    @pl.when(nrows_ref[i] > half)
    def _():
        experts(slice(0, tm))

    @pl.when((nrows_ref[i] > 0) & (nrows_ref[i] <= half))
    def _():
        experts(slice(0, half))
        ys_ref[half:, :] = jnp.zeros((tm - half, D_MODEL), f32)

    @pl.when(nrows_ref[i] == 0)
    def _():
        ys_ref[...] = jnp.zeros_like(ys_ref)


def _moe(xs, tile_grp, tile_a, tile_b, tile_rows, w1, w3, w2, layer):
    n_slots = xs.shape[0]
    tm = MOE_TM
    group_w = lambda shape: pl.BlockSpec((None, None, EXPERTS_PER_GROUP) + shape,
                                         lambda i, g, a, b, va: (layer, g[i], 0, 0, 0),
                                         pipeline_mode=pl.Buffered(1))
    up, down = (D_MODEL, D_EXPERT), (D_EXPERT, D_MODEL)
    grid_spec = pltpu.PrefetchScalarGridSpec(
        num_scalar_prefetch=4,
        grid=(n_slots // tm,),
        in_specs=[pl.BlockSpec((tm, ROW_W), lambda i, g, a, b, va: (i, 0)),
                  group_w(up), group_w(up), group_w(down)],
        out_specs=pl.BlockSpec((tm, D_MODEL), lambda i, g, a, b, va: (i, 0)),
    )
    return pl.pallas_call(
        _moe_kernel,
        grid_spec=grid_spec,
        out_shape=jax.ShapeDtypeStruct((n_slots, D_MODEL), f32),
        compiler_params=_params(1),
        name="moe_experts",
    )(tile_grp, tile_a, tile_b, tile_rows, xs, w1, w3, w2)


def _combine_kernel(pos_ref, pos_next_ref, x_ref, ys_ref, g_ref, b_ref, o_ref, buf, sem):
    i = pl.program_id(0)
    n = x_ref.shape[0]
    slot = i % 2

    def copy(p_ref, s):
        def make(base, u):
            src = ys_ref.at[pl.ds(p_ref[0, 0, base + u], 1), :]
            return pltpu.make_async_copy(src, _tile_row(buf.at[s], base, u), sem.at[s])
        return make

    @pl.when(i == 0)
    def _():
        _for_rows(n, lambda base, u: copy(pos_ref, slot)(base, u).start())

    @pl.when(i + 1 < pl.num_programs(0))
    def _():
        _for_rows(n, lambda base, u: copy(pos_next_ref, 1 - slot)(base, u).start())

    _for_rows(n, lambda base, u: copy(pos_ref, slot)(base, u).wait())
    o_ref[...] = _layer_norm(ALPHA * x_ref[...] + buf[slot], g_ref[...], b_ref[...])


def _combine_ln(rows, ys, pos3, g, b, t0, nt):
    tm = pos3.shape[2]
    vec = pl.BlockSpec((1, D_MODEL), lambda i: (0, 0))
    return pl.pallas_call(
        _combine_kernel,
        grid=(nt,),
        in_specs=[pl.BlockSpec((1, 1, tm), lambda i: (i + t0, 0, 0), memory_space=pltpu.SMEM),
                  pl.BlockSpec((1, 1, tm), lambda i: (jnp.minimum(i + 1, nt - 1) + t0, 0, 0),
                               memory_space=pltpu.SMEM),
                  pl.BlockSpec((tm, D_MODEL), lambda i: (i + t0, 0)),
                  pl.BlockSpec(memory_space=pl.ANY), vec, vec],
        out_specs=pl.BlockSpec((tm, D_MODEL), lambda i: (i, 0)),
        out_shape=jax.ShapeDtypeStruct((nt * tm, D_MODEL), f32),
        scratch_shapes=[pltpu.VMEM((2, tm, D_MODEL), f32), pltpu.SemaphoreType.DMA((2,))],
        compiler_params=_params(1),
        name="moe_combine_ln",
    )(pos3, pos3, rows, ys, g, b)


def _plan(cls, n_tiles):
    T = cls.shape[0]
    blk = PLAN_BLOCK
    onehot = cls[:, None] == jnp.arange(N_CLASSES, dtype=jnp.int32)[None, :]
    oh3 = onehot.astype(bf16).reshape(T // blk, blk, N_CLASSES)
    tri = (np.arange(blk)[:, None] >= np.arange(blk)[None, :]).astype(np.float32)
    within = jnp.einsum("ij,bjc->bic", jnp.asarray(tri, bf16), oh3, preferred_element_type=f32)
    blk_tot = within[:, -1, :]
    blk_end = jnp.cumsum(blk_tot, axis=0)
    csum = (within + (blk_end - blk_tot)[:, None, :]).reshape(T, N_CLASSES)
    rank = jnp.sum(jnp.where(onehot, csum, 0.0), axis=1).astype(jnp.int32) - 1
    counts = blk_end[-1].astype(jnp.int32)
    tiles = (counts + MOE_TM - 1) // MOE_TM
    tile_end = jnp.cumsum(tiles)
    tile_start = tile_end - tiles
    pos = jnp.sum(jnp.where(onehot, tile_start[None, :], 0), axis=1) * MOE_TM + rank
    tile_id = jnp.arange(n_tiles, dtype=jnp.int32)
    used = tile_end[-1]
    tcls = jnp.searchsorted(tile_end, jnp.minimum(tile_id, used - 1), side="right").astype(jnp.int32)
    grp = tcls // PAIRS_PER_GROUP
    pair = tcls % PAIRS_PER_GROUP
    pa, pb = np.triu_indices(EXPERTS_PER_GROUP, 1)
    la = jnp.asarray(pa, jnp.int32)[pair]
    lb = jnp.asarray(pb, jnp.int32)[pair]
    nrows = jnp.clip(counts[tcls] - (tile_id - tile_start[tcls]) * MOE_TM, 0, MOE_TM)
    nrows = jnp.where(tile_id < used, nrows, 0).astype(jnp.int32)
    return pos.astype(jnp.int32), grp, la, lb, nrows


def _t5_buckets(rel):
    nb = N_BUCKETS // 2
    max_exact = nb // 2
    ret = np.where(rel > 0, nb, 0)
    n = np.abs(rel)
    nf = np.maximum(n, 1).astype(np.float32)
    large = max_exact + (np.log(nf / np.float32(max_exact)) / np.float32(math.log(T5_MAX_DISTANCE / max_exact))
                         * np.float32(nb - max_exact)).astype(np.int32)
    large = np.minimum(large, nb - 1)
    return ret + np.where(n < max_exact, n, large)


def _table_lookup(table, idx):
    onehot = (jnp.asarray(idx, jnp.int32)[..., None] == jnp.arange(table.shape[0], dtype=jnp.int32)).astype(f32)
    return jnp.einsum("...n,nh->...h", onehot, table, precision=lax.Precision.HIGHEST)


def _dilated_bias(t5_table, dil):
    qi = np.arange(QBLK)[:, None]
    kj = np.arange(KBLK)[None, :]
    out = []
    for shift in (0, HALF, 2 * HALF):
        rel = kj - qi - shift
        band = np.abs(rel) <= HALF
        vals = _table_lookup(t5_table.astype(f32), _t5_buckets(rel * dil))
        vals = jnp.where(jnp.asarray(band)[:, :, None], vals, NEG_INF)
        out.append(vals.transpose(2, 0, 1))
    tab = jnp.stack(out, axis=1)
    tab = tab.reshape(HEAD_PAIRS, 2, 3, QBLK, KBLK).transpose(0, 2, 1, 3, 4)
    return tab.reshape(HEAD_PAIRS, 3, 2 * QBLK, KBLK)


def _neighbourhood_bias(rpb, rows):
    kh = min(NA_KH, rows)
    nvar = NA_KH if rows > kh else 1
    j = np.arange(GRID_W)[:, None]
    kc = np.arange(GRID_W)[None, :]
    ws = np.clip(j - NA_KW // 2, 0, GRID_W - NA_KW)
    inwin = (kc >= ws) & (kc < ws + NA_KW)
    dc = np.clip(kc - j + NA_KW - 1, 0, 2 * NA_KW - 2)
    ndr, ndc = rpb.shape[1], rpb.shape[2]
    cols = _table_lookup(rpb.astype(f32).reshape(N_HEADS * ndr, ndc).T, dc)
    cols = jnp.where(jnp.asarray(inwin)[:, :, None], cols, NEG_INF)
    cols = cols.reshape(GRID_W, GRID_W, N_HEADS, ndr).transpose(2, 3, 0, 1)
    out = []
    for var in range(nvar):
        lo = var + (NA_KH - kh)
        vals = cols[:, lo:lo + kh]
        out.append(vals.transpose(0, 2, 1, 3).reshape(N_HEADS, GRID_W, kh * GRID_W))
    tab = jnp.stack(out, axis=1)
    tab = tab.reshape(HEAD_PAIRS, 2, nvar, GRID_W, kh * GRID_W).transpose(0, 2, 1, 3, 4)
    return tab.reshape(HEAD_PAIRS, nvar, 2 * GRID_W, kh * GRID_W)


def _qkv_weight(w):
    hd = N_HEADS * HEAD_DIM
    scale = jnp.concatenate([jnp.full((hd,), HEAD_DIM ** -0.5, f32), jnp.ones((2 * hd,), f32)])
    return (w * scale[None, :]).astype(bf16)


def _router_weight(w_rg, b_rg, w_re, b_re):
    pad = LANES - N_GROUPS - N_EXPERTS
    wr = jnp.concatenate([w_rg, w_re, jnp.zeros((D_MODEL, pad), f32)], axis=1)
    br = jnp.concatenate([b_rg, b_re, jnp.zeros((pad,), f32)])[None, :]
    hi = wr.astype(bf16)
    lo = (wr - hi.astype(f32)).astype(bf16)
    return jnp.concatenate([hi, lo], axis=1), br


def _trunk(x_parts, t5_table, w_qkv_a, w_o_a, w_qkv_b, w_o_b, rpb_b, ln_g, ln_b,
           w_rg, b_rg, w_re, b_re, w1, w3, w2):
    _, S, D = x_parts[0].shape
    batches = [x.shape[0] for x in x_parts]
    B = sum(batches)
    T = B * S
    n_slots = T + N_CLASSES * MOE_TM
    n_tiles = n_slots // MOE_TM
    hd = N_HEADS * HEAD_DIM
    by_group = lambda w: w.astype(bf16).reshape(DEPTH, N_GROUPS, EXPERTS_PER_GROUP, *w.shape[2:])
    w1b, w3b, w2b = by_group(w1), by_group(w3), by_group(w2)
    bias_a = [_dilated_bias(t5_table, dil) for _, dil in DILATIONS]
    xs = jnp.zeros((n_slots, ROW_W), f32)
    x3s = list(x_parts)
    for i in range(DEPTH):
        j = i // 2
        if i % 2 == 0:
            qkvs = [_qkv_proj(x3s, _qkv_weight(w_qkv_a[j][:, g * 3 * hd:(g + 1) * 3 * hd]), dil)
                    for g, (_, dil) in enumerate(DILATIONS)]
            att = _attn_a(qkvs, bias_a)
            wo = w_o_a[j]
        else:
            qkv = _qkv_proj(x3s, _qkv_weight(w_qkv_b[j]), 1)
            att = _attn_b(qkv, _neighbourhood_bias(rpb_b[j], S // GRID_W))
            wo = w_o_b[j]
        wr, br = _router_weight(w_rg[i], b_rg[i], w_re[i], b_re[i])
        rows = _oproj_ln_route(att, [x.reshape(-1, D) for x in x3s], wo.astype(bf16),
                               ln_g[i, 0][None, :], ln_b[i, 0][None, :], wr, br)
        cls = rows[:, D_MODEL + 2].astype(jnp.int32)
        pos, grp, la, lb, nrows = _plan(cls, n_tiles)
        pos3 = pos.reshape(T // ROW_DMA_TM, 1, ROW_DMA_TM)
        xs = _dispatch(rows, pos3, xs)
        ys = _moe(xs, grp, la, lb, nrows, w1b, w3b, w2b, i)
        out_batches = batches if i == DEPTH - 1 else [B]
        x3s, t0 = [], 0
        for nb in out_batches:
            nt = nb * S // ROW_DMA_TM
            y = _combine_ln(rows, ys, pos3, ln_g[i, 1][None, :], ln_b[i, 1][None, :], t0, nt)
            x3s.append(y.reshape(nb, S, D))
            t0 += nt
    return x3s


def kernel(x_prompt, x_sample, t5_table, w_qkv_a, w_o_a, w_qkv_b, w_o_b, rpb_b, ln_g, ln_b,
           w_rg, b_rg, w_re, b_re, w1, w3, w2):
    y_prompt, y_sample = _trunk([x_prompt, x_sample], t5_table, w_qkv_a, w_o_a, w_qkv_b, w_o_b, rpb_b,
                                ln_g, ln_b, w_rg, b_rg, w_re, b_re, w1, w3, w2)
    return (y_prompt, y_sample)
```

```python
import functools
import math

import numpy as np
import jax
import jax.numpy as jnp
from jax import lax
from jax.experimental import pallas as pl
from jax.experimental.pallas import tpu as pltpu

D_MODEL = 1024
SEQ = 4096
DEPTH = 4
HEAD_DIM = 64
N_HEADS = 16
DILATIONS = ((128, 1), (512, 4), (2048, 16))
N_BUCKETS = 32
T5_MAX_DISTANCE = 1024
GRID_W = 64
NA_KH = 8
NA_KW = 16
N_GROUPS = 4
EXPERTS_PER_GROUP = 8
N_EXPERTS = N_GROUPS * EXPERTS_PER_GROUP
D_EXPERT = 512
ALPHA = (2 * DEPTH) ** 0.25
LN_EPS = 1e-5
NEG_INF = -1e30

LANES = 128
SUBLANES = 8
HEAD_PAIRS = N_HEADS * HEAD_DIM // LANES
QKV_BLOCKS = 3 * HEAD_PAIRS
VMEM_LIMIT = 56 * 1024 * 1024

QBLK = 128
KBLK = 256
HALF = 64
NA_ROWS_PER_ITER = 32
DIL_BLOCKS_PER_ITER = 16
PAIRS_PER_GROUP = EXPERTS_PER_GROUP * (EXPERTS_PER_GROUP - 1) // 2
N_CLASSES = N_GROUPS * PAIRS_PER_GROUP
ROW_W = D_MODEL + LANES
MOE_TM = 256
PLAN_BLOCK = 512
TOK_TM = 1024
ROW_DMA_TM = 1024
DMA_UNROLL = 8

f32 = jnp.float32
bf16 = jnp.bfloat16


def _params(n_grid):
    return pltpu.CompilerParams(dimension_semantics=("arbitrary",) * n_grid,
                                vmem_limit_bytes=VMEM_LIMIT)


def _select_source(refs, split, read):
    if len(refs) == 1:
        return read(refs[0])
    return jnp.where(pl.program_id(0) < split, read(refs[0]), read(refs[1]))


def _source_index(k, n_src, split):
    if n_src == 1:
        return lambda b: b
    return (lambda b: jnp.minimum(b, split - 1)) if k == 0 else (lambda b: jnp.maximum(b - split, 0))


def _qkv_kernel(*refs, dil, tm, cw, n_src, split):
    x_refs = refs[:n_src]
    w_ref, o_ref, acc_ref = refs[n_src:]
    nsub = cw // LANES
    rows = tm // dil
    rows_in = len(x_refs[0].shape) == 3
    if rows_in:
        xb = jnp.concatenate([_select_source(x_refs, split, lambda ref, r=r: ref[:, r, :]) for r in range(dil)],
                             axis=0).astype(bf16)
    else:
        xb = _select_source(x_refs, split, lambda ref: ref[...]).astype(bf16)
    for j in range(QKV_BLOCKS // nsub):
        res = jnp.dot(xb, w_ref[:, j * cw:(j + 1) * cw], preferred_element_type=f32)
        if dil == 1 or rows_in:
            for kk in range(nsub):
                for r in range(dil):
                    o_ref[j * nsub + kk, r, :, :] = res[r * rows:(r + 1) * rows,
                                                        kk * LANES:(kk + 1) * LANES].astype(bf16)
            continue
        for kk in range(nsub):
            acc_ref[kk] = res[:, kk * LANES:(kk + 1) * LANES]
        for kk in range(nsub):
            for r in range(dil):
                o_ref[j * nsub + kk, r, :, :] = acc_ref[kk, pl.ds(r, rows, stride=dil), :].astype(bf16)


def _qkv_proj(xs, w, dil):
    _, S, D = xs[0].shape
    n_src, split = len(xs), xs[0].shape[0]
    B = sum(x.shape[0] for x in xs)
    tm, cw = 1024, 512
    L = S // dil
    in_specs, args = [], []
    for k, x3 in enumerate(xs):
        src = _source_index(k, n_src, split)
        if dil % SUBLANES == 0:
            x3 = x3.reshape(x3.shape[0], L, dil, D)
            in_specs.append(pl.BlockSpec((None, tm // dil, dil, D), lambda b, i, src=src: (src(b), i, 0, 0)))
        else:
            in_specs.append(pl.BlockSpec((None, tm, D), lambda b, i, src=src: (src(b), i, 0)))
        args.append(x3)
    in_specs.append(pl.BlockSpec((D, QKV_BLOCKS * LANES), lambda b, i: (0, 0)))
    return pl.pallas_call(
        functools.partial(_qkv_kernel, dil=dil, tm=tm, cw=cw, n_src=n_src, split=split),
        grid=(B, S // tm),
        in_specs=in_specs,
        out_specs=pl.BlockSpec((None, QKV_BLOCKS, dil, tm // dil, LANES), lambda b, i: (b, 0, 0, i, 0)),
        out_shape=jax.ShapeDtypeStruct((B, QKV_BLOCKS, dil, L, LANES), bf16),
        scratch_shapes=[pltpu.VMEM((cw // LANES, tm, LANES), f32)],
        compiler_params=_params(2),
        name=f"qkv_proj_d{dil}",
    )(*args, w)


def _softmax_block(qst, kb, vb, bias):
    nq = qst.shape[0] // 2
    s = lax.dot_general(qst, kb, (((1,), (1,)), ((), ())), preferred_element_type=f32) + bias
    m = jnp.max(s, axis=1, keepdims=True)
    p = jnp.exp((s - m).astype(bf16))
    v1 = jnp.concatenate([vb, jnp.ones_like(vb)], axis=1)
    ol = jnp.dot(p, v1, preferred_element_type=f32)
    l = ol[:, LANES:]
    o2 = ol[:, :LANES] * (1.0 / l)
    lse = m + jnp.log(l)
    lane = lax.broadcasted_iota(jnp.int32, (nq, LANES), 1)
    o = jnp.where(lane < HEAD_DIM, o2[:nq], o2[nq:])
    ls = jnp.where(lane < HEAD_DIM, lse[:nq], lse[nq:])
    return o, ls


def _stack_heads(q2):
    lane = lax.broadcasted_iota(jnp.int32, q2.shape, 1)
    zero = jnp.zeros_like(q2)
    return jnp.concatenate([jnp.where(lane < HEAD_DIM, q2, zero),
                            jnp.where(lane >= HEAD_DIM, q2, zero)], axis=0)


def _attn_a_kernel(*refs):
    qkv = refs[:9]
    biases = refs[9:12]
    o_ref = refs[12]
    oscr, lscr = refs[13], refs[14]
    S = o_ref.shape[0]

    def block(it, g):
        dil = DILATIONS[g][1]
        q_ref, k_ref, v_ref = qkv[3 * g:3 * g + 3]
        L = S // dil
        nb = L // QBLK
        r = it // nb
        n = it % nb
        qs = pl.multiple_of(n * QBLK, QBLK)
        ks = pl.multiple_of(jnp.clip(qs - HALF, 0, L - KBLK), HALF)
        var = jnp.where(n == 0, 0, jnp.where(n == nb - 1, 2, 1))
        qst = _stack_heads(q_ref[r, pl.ds(qs, QBLK), :])
        o, ls = _softmax_block(qst, k_ref[r, pl.ds(ks, KBLK), :], v_ref[r, pl.ds(ks, KBLK), :], biases[g][var])
        if dil == 1:
            dst = pl.ds(qs, QBLK)
        else:
            dst = pl.ds(qs * dil + r, QBLK, stride=dil)
        oscr[g, dst, :] = o
        lscr[g, dst, :] = ls

    def body(it, carry):
        for u in range(DIL_BLOCKS_PER_ITER):
            for g in range(len(DILATIONS)):
                block(it * DIL_BLOCKS_PER_ITER + u, g)
        return carry

    lax.fori_loop(0, S // QBLK // DIL_BLOCKS_PER_ITER, body, 0)

    ch = 256

    def merge(c, carry):
        rows = pl.ds(pl.multiple_of(c * ch, ch), ch)
        l0, l1, l2 = lscr[0, rows, :], lscr[1, rows, :], lscr[2, rows, :]
        mx = jnp.maximum(jnp.maximum(l0, l1), l2)
        e0, e1, e2 = jnp.exp(l0 - mx), jnp.exp(l1 - mx), jnp.exp(l2 - mx)
        inv = 1.0 / (e0 + e1 + e2)
        o = (e0 * oscr[0, rows, :] + e1 * oscr[1, rows, :] + e2 * oscr[2, rows, :]) * inv
        o_ref[rows, :] = o.astype(bf16)
        return carry

    lax.fori_loop(0, S // ch, merge, 0)


def _attn_a(qkvs, biases):
    B = qkvs[0].shape[0]
    S = qkvs[0].shape[2] * qkvs[0].shape[3]
    in_specs, args = [], []
    for g, (_, dil) in enumerate(DILATIONS):
        L = S // dil
        for c in range(3):
            in_specs.append(pl.BlockSpec((None, None, dil, L, LANES),
                                         lambda b, hp, c=c: (b, c * HEAD_PAIRS + hp, 0, 0, 0)))
            args.append(qkvs[g])
    for g in range(3):
        in_specs.append(pl.BlockSpec((None, 3, 2 * QBLK, KBLK), lambda b, hp: (hp, 0, 0, 0)))
        args.append(biases[g])
    return pl.pallas_call(
        _attn_a_kernel,
        grid=(B, HEAD_PAIRS),
        in_specs=in_specs,
        out_specs=pl.BlockSpec((None, None, S, LANES), lambda b, hp: (b, hp, 0, 0)),
        out_shape=jax.ShapeDtypeStruct((B, HEAD_PAIRS, S, LANES), bf16),
        scratch_shapes=[pltpu.VMEM((3, S, LANES), f32), pltpu.VMEM((3, S, LANES), f32)],
        compiler_params=_params(2),
        name="dilated_attn",
    )(*args)


def _attn_b_kernel(q_ref, k_ref, v_ref, b_ref, o_ref):
    rows = o_ref.shape[0] // GRID_W
    kh = min(NA_KH, rows)
    nk = kh * GRID_W

    def one_row(i):
        rs = jnp.clip(i - kh // 2, 0, rows - kh)
        var = rs - i + NA_KH - 1 - (NA_KH - kh)
        qs = pl.multiple_of(i * GRID_W, GRID_W)
        ks = pl.multiple_of(rs * GRID_W, GRID_W)
        qst = _stack_heads(q_ref[pl.ds(qs, GRID_W), :])
        o, _ = _softmax_block(qst, k_ref[pl.ds(ks, nk), :], v_ref[pl.ds(ks, nk), :], b_ref[var])
        o_ref[pl.ds(qs, GRID_W), :] = o.astype(bf16)

    def body(it, carry):
        for u in range(NA_ROWS_PER_ITER):
            one_row(it * NA_ROWS_PER_ITER + u)
        return carry

    lax.fori_loop(0, rows // NA_ROWS_PER_ITER, body, 0)


def _attn_b(qkv, bias):
    B, _, _, S, _ = qkv.shape
    nvar, nk = bias.shape[1], bias.shape[3]
    in_specs = [pl.BlockSpec((None, None, None, S, LANES), lambda b, hp, c=c: (b, c * HEAD_PAIRS + hp, 0, 0, 0))
                for c in range(3)]
    in_specs.append(pl.BlockSpec((None, nvar, 2 * GRID_W, nk), lambda b, hp: (hp, 0, 0, 0)))
    return pl.pallas_call(
        _attn_b_kernel,
        grid=(B, HEAD_PAIRS),
        in_specs=in_specs,
        out_specs=pl.BlockSpec((None, None, S, LANES), lambda b, hp: (b, hp, 0, 0)),
        out_shape=jax.ShapeDtypeStruct((B, HEAD_PAIRS, S, LANES), bf16),
        compiler_params=_params(2),
        name="neighbourhood_attn",
    )(qkv, qkv, qkv, bias)


def _layer_norm(y, g, b):
    mu = jnp.mean(y, axis=1, keepdims=True)
    yc = y - mu
    var = jnp.mean(yc * yc, axis=1, keepdims=True)
    return yc * lax.rsqrt(var + LN_EPS) * g + b


def _route(logits):
    shape = logits.shape
    lane = lax.broadcasted_iota(jnp.int32, shape, 1)
    big = jnp.int32(4 * LANES)
    is_g = lane < N_GROUPS
    gl = jnp.where(is_g, logits, NEG_INF)
    mg = jnp.max(gl, axis=1, keepdims=True)
    gsel = jnp.min(jnp.where(gl == mg, lane, big), axis=1, keepdims=True)
    p_sel = 1.0 / jnp.sum(jnp.where(is_g, jnp.exp(gl - mg), 0.0), axis=1, keepdims=True)
    lo = N_GROUPS + gsel * EXPERTS_PER_GROUP
    el = jnp.where((lane >= lo) & (lane < lo + EXPERTS_PER_GROUP), logits, NEG_INF)
    v1 = jnp.max(el, axis=1, keepdims=True)
    i1 = jnp.min(jnp.where(el == v1, lane, big), axis=1, keepdims=True)
    el2 = jnp.where(lane == i1, NEG_INF, el)
    v2 = jnp.max(el2, axis=1, keepdims=True)
    i2 = jnp.min(jnp.where(el2 == v2, lane, big), axis=1, keepdims=True)
    t = jnp.exp(v2 - v1)
    g1 = p_sel / (1.0 + t)
    g2 = p_sel * t / (1.0 + t)
    first_lo = i1 < i2
    a = jnp.where(first_lo, i1, i2) - lo
    b = jnp.where(first_lo, i2, i1) - lo
    ga = jnp.where(first_lo, g1, g2)
    gb = jnp.where(first_lo, g2, g1)
    pair = (EXPERTS_PER_GROUP - 1) * a - ((a * (a - 1)) >> 1) + (b - a - 1)
    cls = (gsel * PAIRS_PER_GROUP + pair).astype(f32)
    return jnp.where(lane == 0, ga, jnp.where(lane == 1, gb, jnp.where(lane == 2, cls, 0.0)))


def _oproj_kernel(*refs, n_src, split):
    att_ref = refs[0]
    x_refs = refs[1:1 + n_src]
    wo_ref, g_ref, b_ref, wr_ref, br_ref, o_ref = refs[1 + n_src:]
    a = jnp.concatenate([att_ref[h] for h in range(HEAD_PAIRS)], axis=1)
    h = jnp.dot(a, wo_ref[...], preferred_element_type=f32)
    x = _select_source(x_refs, split, lambda ref: ref[...])
    xn = _layer_norm(ALPHA * x + h, g_ref[...], b_ref[...])
    xh = xn.astype(bf16)
    xl = (xn - xh.astype(f32)).astype(bf16)
    hw = jnp.dot(xh, wr_ref[...], preferred_element_type=f32)
    lw = jnp.dot(xl, wr_ref[:, :LANES], preferred_element_type=f32)
    logits = hw[:, :LANES] + hw[:, LANES:] + lw + br_ref[...]
    o_ref[:, :D_MODEL] = xn
    o_ref[:, D_MODEL:] = _route(logits)


def _oproj_ln_route(att, xs, wo, g, b, wr, br):
    B, _, S, _ = att.shape
    T = B * S
    tm = TOK_TM
    per = S // tm
    n_src, split = len(xs), xs[0].shape[0] // tm
    vec = lambda n: pl.BlockSpec((1, n), lambda i: (0, 0))
    x_specs = [pl.BlockSpec((tm, D_MODEL), lambda i, src=_source_index(k, n_src, split): (src(i), 0))
               for k in range(n_src)]
    return pl.pallas_call(
        functools.partial(_oproj_kernel, n_src=n_src, split=split),
        grid=(T // tm,),
        in_specs=[pl.BlockSpec((None, HEAD_PAIRS, tm, LANES), lambda i: (i // per, 0, i % per, 0))] + x_specs
                 + [pl.BlockSpec((D_MODEL, D_MODEL), lambda i: (0, 0)), vec(D_MODEL), vec(D_MODEL),
                    pl.BlockSpec((D_MODEL, 2 * LANES), lambda i: (0, 0)), vec(LANES)],
        out_specs=pl.BlockSpec((tm, ROW_W), lambda i: (i, 0)),
        out_shape=jax.ShapeDtypeStruct((T, ROW_W), f32),
        compiler_params=_params(1),
        name="oproj_ln_route",
    )(att, *xs, wo, g, b, wr, br)


def _for_rows(n, fn):
    def body(c, carry):
        base = pl.multiple_of(c * DMA_UNROLL, DMA_UNROLL)
        for u in range(DMA_UNROLL):
            fn(base, u)
        return carry

    lax.fori_loop(0, n // DMA_UNROLL, body, 0)


def _tile_row(ref, base, u):
    return ref.at[pl.ds(base, DMA_UNROLL), :].at[pl.ds(u, 1), :]


def _dispatch_kernel(pos_ref, rows_ref, init_ref, xs_ref, sem):
    del init_ref

    def copy(base, u):
        dst = xs_ref.at[pl.ds(pos_ref[0, 0, base + u], 1), :]
        return pltpu.make_async_copy(_tile_row(rows_ref, base, u), dst, sem)

    n = rows_ref.shape[0]
    _for_rows(n, lambda base, u: copy(base, u).start())
    _for_rows(n, lambda base, u: copy(base, u).wait())


def _dispatch(rows, pos3, slots):
    T = rows.shape[0]
    tm = pos3.shape[2]
    return pl.pallas_call(
        _dispatch_kernel,
        grid=(T // tm,),
        in_specs=[pl.BlockSpec((1, 1, tm), lambda i: (i, 0, 0), memory_space=pltpu.SMEM),
                  pl.BlockSpec((tm, ROW_W), lambda i: (i, 0)),
                  pl.BlockSpec(memory_space=pl.ANY)],
        out_specs=pl.BlockSpec(memory_space=pl.ANY),
        out_shape=jax.ShapeDtypeStruct(slots.shape, f32),
        scratch_shapes=[pltpu.SemaphoreType.DMA(())],
        input_output_aliases={2: 0},
        compiler_params=_params(1),
        name="moe_dispatch",
    )(pos3, rows, slots)


def _swiglu(xt, w1, w3, w2):
    h1 = jnp.dot(xt, w1, preferred_element_type=f32)
    h3 = jnp.dot(xt, w3, preferred_element_type=f32)
    h = h1 * (1.0 / (1.0 + jnp.exp(-h1))) * h3
    return jnp.dot(h.astype(bf16), w2, preferred_element_type=f32)


def _moe_kernel(grp_ref, la_ref, lb_ref, nrows_ref, xs_ref, w1_ref, w3_ref, w2_ref, ys_ref):
    del grp_ref
    i = pl.program_id(0)
    tm = xs_ref.shape[0]
    half = tm // 2

    def experts(rows):
        a, b = la_ref[i], lb_ref[i]
        xt = xs_ref[rows, :D_MODEL].astype(bf16)
        ga = xs_ref[rows, D_MODEL:D_MODEL + 1]
        gb = xs_ref[rows, D_MODEL + 1:D_MODEL + 2]
        ys_ref[rows, :] = (ga * _swiglu(xt, w1_ref[a], w3_ref[a], w2_ref[a])
                           + gb * _swiglu(xt, w1_ref[b], w3_ref[b], w2_ref[b]))

    @pl.when(nrows_ref[i] > half)
    def _():
        experts(slice(0, tm))

    @pl.when((nrows_ref[i] > 0) & (nrows_ref[i] <= half))
    def _():
        experts(slice(0, half))
        ys_ref[half:, :] = jnp.zeros((tm - half, D_MODEL), f32)

    @pl.when(nrows_ref[i] == 0)
    def _():
        ys_ref[...] = jnp.zeros_like(ys_ref)


def _moe(xs, tile_grp, tile_a, tile_b, tile_rows, w1, w3, w2, layer):
    n_slots = xs.shape[0]
    tm = MOE_TM
    group_w = lambda shape: pl.BlockSpec((None, None, EXPERTS_PER_GROUP) + shape,
                                         lambda i, g, a, b, va: (layer, g[i], 0, 0, 0),
                                         pipeline_mode=pl.Buffered(1))
    up, down = (D_MODEL, D_EXPERT), (D_EXPERT, D_MODEL)
    grid_spec = pltpu.PrefetchScalarGridSpec(
        num_scalar_prefetch=4,
        grid=(n_slots // tm,),
        in_specs=[pl.BlockSpec((tm, ROW_W), lambda i, g, a, b, va: (i, 0)),
                  group_w(up), group_w(up), group_w(down)],
        out_specs=pl.BlockSpec((tm, D_MODEL), lambda i, g, a, b, va: (i, 0)),
    )
    return pl.pallas_call(
        _moe_kernel,
        grid_spec=grid_spec,
        out_shape=jax.ShapeDtypeStruct((n_slots, D_MODEL), f32),
        compiler_params=_params(1),
        name="moe_experts",
    )(tile_grp, tile_a, tile_b, tile_rows, xs, w1, w3, w2)


def _combine_kernel(pos_ref, pos_next_ref, x_ref, ys_ref, g_ref, b_ref, o_ref, buf, sem):
    i = pl.program_id(0)
    n = x_ref.shape[0]
    slot = i % 2

    def copy(p_ref, s):
        def make(base, u):
            src = ys_ref.at[pl.ds(p_ref[0, 0, base + u], 1), :]
            return pltpu.make_async_copy(src, _tile_row(buf.at[s], base, u), sem.at[s])
        return make

    @pl.when(i == 0)
    def _():
        _for_rows(n, lambda base, u: copy(pos_ref, slot)(base, u).start())

    @pl.when(i + 1 < pl.num_programs(0))
    def _():
        _for_rows(n, lambda base, u: copy(pos_next_ref, 1 - slot)(base, u).start())

    _for_rows(n, lambda base, u: copy(pos_ref, slot)(base, u).wait())
    o_ref[...] = _layer_norm(ALPHA * x_ref[...] + buf[slot], g_ref[...], b_ref[...])


def _combine_ln(rows, ys, pos3, g, b, t0, nt):
    tm = pos3.shape[2]
    vec = pl.BlockSpec((1, D_MODEL), lambda i: (0, 0))
    return pl.pallas_call(
        _combine_kernel,
        grid=(nt,),
        in_specs=[pl.BlockSpec((1, 1, tm), lambda i: (i + t0, 0, 0), memory_space=pltpu.SMEM),
                  pl.BlockSpec((1, 1, tm), lambda i: (jnp.minimum(i + 1, nt - 1) + t0, 0, 0),
                               memory_space=pltpu.SMEM),
                  pl.BlockSpec((tm, D_MODEL), lambda i: (i + t0, 0)),
                  pl.BlockSpec(memory_space=pl.ANY), vec, vec],
        out_specs=pl.BlockSpec((tm, D_MODEL), lambda i: (i, 0)),
        out_shape=jax.ShapeDtypeStruct((nt * tm, D_MODEL), f32),
        scratch_shapes=[pltpu.VMEM((2, tm, D_MODEL), f32), pltpu.SemaphoreType.DMA((2,))],
        compiler_params=_params(1),
        name="moe_combine_ln",
    )(pos3, pos3, rows, ys, g, b)


def _plan(cls, n_tiles):
    T = cls.shape[0]
    blk = PLAN_BLOCK
    onehot = cls[:, None] == jnp.arange(N_CLASSES, dtype=jnp.int32)[None, :]
    oh3 = onehot.astype(bf16).reshape(T // blk, blk, N_CLASSES)
    tri = (np.arange(blk)[:, None] >= np.arange(blk)[None, :]).astype(np.float32)
    within = jnp.einsum("ij,bjc->bic", jnp.asarray(tri, bf16), oh3, preferred_element_type=f32)
    blk_tot = within[:, -1, :]
    blk_end = jnp.cumsum(blk_tot, axis=0)
    csum = (within + (blk_end - blk_tot)[:, None, :]).reshape(T, N_CLASSES)
    rank = jnp.sum(jnp.where(onehot, csum, 0.0), axis=1).astype(jnp.int32) - 1
    counts = blk_end[-1].astype(jnp.int32)
    tiles = (counts + MOE_TM - 1) // MOE_TM
    tile_end = jnp.cumsum(tiles)
    tile_start = tile_end - tiles
    pos = jnp.sum(jnp.where(onehot, tile_start[None, :], 0), axis=1) * MOE_TM + rank
    tile_id = jnp.arange(n_tiles, dtype=jnp.int32)
    used = tile_end[-1]
    tcls = jnp.sum(jnp.minimum(tile_id, used - 1)[:, None] >= tile_end[None, :], axis=1).astype(jnp.int32)
    tile_onehot = tcls[:, None] == jnp.arange(N_CLASSES, dtype=jnp.int32)[None, :]
    lookup = lambda table: jnp.sum(jnp.where(tile_onehot, jnp.asarray(table, jnp.int32)[None, :], 0), axis=1)
    pa, pb = np.triu_indices(EXPERTS_PER_GROUP, 1)
    grp = tcls // PAIRS_PER_GROUP
    la = lookup(np.tile(pa, N_GROUPS))
    lb = lookup(np.tile(pb, N_GROUPS))
    nrows = jnp.clip(lookup(counts) - (tile_id - lookup(tile_start)) * MOE_TM, 0, MOE_TM)
    nrows = jnp.where(tile_id < used, nrows, 0).astype(jnp.int32)
    return pos.astype(jnp.int32), grp, la, lb, nrows


def _t5_buckets(rel):
    nb = N_BUCKETS // 2
    max_exact = nb // 2
    ret = np.where(rel > 0, nb, 0)
    n = np.abs(rel)
    nf = np.maximum(n, 1).astype(np.float32)
    large = max_exact + (np.log(nf / np.float32(max_exact)) / np.float32(math.log(T5_MAX_DISTANCE / max_exact))
                         * np.float32(nb - max_exact)).astype(np.int32)
    large = np.minimum(large, nb - 1)
    return ret + np.where(n < max_exact, n, large)


def _table_lookup(table, idx):
    onehot = (jnp.asarray(idx, jnp.int32)[..., None] == jnp.arange(table.shape[0], dtype=jnp.int32)).astype(f32)
    return jnp.einsum("...n,nh->...h", onehot, table, precision=lax.Precision.HIGHEST)


def _dilated_bias(t5_table, dil):
    qi = np.arange(QBLK)[:, None]
    kj = np.arange(KBLK)[None, :]
    out = []
    for shift in (0, HALF, 2 * HALF):
        rel = kj - qi - shift
        band = np.abs(rel) <= HALF
        vals = _table_lookup(t5_table.astype(f32), _t5_buckets(rel * dil))
        vals = jnp.where(jnp.asarray(band)[:, :, None], vals, NEG_INF)
        out.append(vals.transpose(2, 0, 1))
    tab = jnp.stack(out, axis=1)
    tab = tab.reshape(HEAD_PAIRS, 2, 3, QBLK, KBLK).transpose(0, 2, 1, 3, 4)
    return tab.reshape(HEAD_PAIRS, 3, 2 * QBLK, KBLK)


def _neighbourhood_bias(rpb, rows):
    kh = min(NA_KH, rows)
    nvar = NA_KH if rows > kh else 1
    j = np.arange(GRID_W)[:, None]
    kc = np.arange(GRID_W)[None, :]
    ws = np.clip(j - NA_KW // 2, 0, GRID_W - NA_KW)
    inwin = (kc >= ws) & (kc < ws + NA_KW)
    dc = np.clip(kc - j + NA_KW - 1, 0, 2 * NA_KW - 2)
    ndr, ndc = rpb.shape[1], rpb.shape[2]
    cols = _table_lookup(rpb.astype(f32).reshape(N_HEADS * ndr, ndc).T, dc)
    cols = jnp.where(jnp.asarray(inwin)[:, :, None], cols, NEG_INF)
    cols = cols.reshape(GRID_W, GRID_W, N_HEADS, ndr).transpose(2, 3, 0, 1)
    out = []
    for var in range(nvar):
        lo = var + (NA_KH - kh)
        vals = cols[:, lo:lo + kh]
        out.append(vals.transpose(0, 2, 1, 3).reshape(N_HEADS, GRID_W, kh * GRID_W))
    tab = jnp.stack(out, axis=1)
    tab = tab.reshape(HEAD_PAIRS, 2, nvar, GRID_W, kh * GRID_W).transpose(0, 2, 1, 3, 4)
    return tab.reshape(HEAD_PAIRS, nvar, 2 * GRID_W, kh * GRID_W)


def _qkv_weight(w):
    hd = N_HEADS * HEAD_DIM
    scale = jnp.concatenate([jnp.full((hd,), HEAD_DIM ** -0.5, f32), jnp.ones((2 * hd,), f32)])
    return (w * scale[None, :]).astype(bf16)


def _router_weight(w_rg, b_rg, w_re, b_re):
    pad = LANES - N_GROUPS - N_EXPERTS
    wr = jnp.concatenate([w_rg, w_re, jnp.zeros((D_MODEL, pad), f32)], axis=1)
    br = jnp.concatenate([b_rg, b_re, jnp.zeros((pad,), f32)])[None, :]
    hi = wr.astype(bf16)
    lo = (wr - hi.astype(f32)).astype(bf16)
    return jnp.concatenate([hi, lo], axis=1), br


def _trunk(x_parts, t5_table, w_qkv_a, w_o_a, w_qkv_b, w_o_b, rpb_b, ln_g, ln_b,
           w_rg, b_rg, w_re, b_re, w1, w3, w2):
    _, S, D = x_parts[0].shape
    batches = [x.shape[0] for x in x_parts]
    B = sum(batches)
    T = B * S
    n_slots = T + N_CLASSES * MOE_TM
    n_tiles = n_slots // MOE_TM
    hd = N_HEADS * HEAD_DIM
    by_group = lambda w: w.astype(bf16).reshape(DEPTH, N_GROUPS, EXPERTS_PER_GROUP, *w.shape[2:])
    w1b, w3b, w2b = by_group(w1), by_group(w3), by_group(w2)
    bias_a = [_dilated_bias(t5_table, dil) for _, dil in DILATIONS]
    xs = jnp.zeros((n_slots, ROW_W), f32)
    x3s = list(x_parts)
    for i in range(DEPTH):
        j = i // 2
        if i % 2 == 0:
            qkvs = [_qkv_proj(x3s, _qkv_weight(w_qkv_a[j][:, g * 3 * hd:(g + 1) * 3 * hd]), dil)
                    for g, (_, dil) in enumerate(DILATIONS)]
            att = _attn_a(qkvs, bias_a)
            wo = w_o_a[j]
        else:
            qkv = _qkv_proj(x3s, _qkv_weight(w_qkv_b[j]), 1)
            att = _attn_b(qkv, _neighbourhood_bias(rpb_b[j], S // GRID_W))
            wo = w_o_b[j]
        wr, br = _router_weight(w_rg[i], b_rg[i], w_re[i], b_re[i])
        rows = _oproj_ln_route(att, [x.reshape(-1, D) for x in x3s], wo.astype(bf16),
                               ln_g[i, 0][None, :], ln_b[i, 0][None, :], wr, br)
        cls = rows[:, D_MODEL + 2].astype(jnp.int32)
        pos, grp, la, lb, nrows = _plan(cls, n_tiles)
        pos3 = pos.reshape(T // ROW_DMA_TM, 1, ROW_DMA_TM)
        xs = _dispatch(rows, pos3, xs)
        ys = _moe(xs, grp, la, lb, nrows, w1b, w3b, w2b, i)
        out_batches = batches if i == DEPTH - 1 else [B]
        x3s, t0 = [], 0
        for nb in out_batches:
            nt = nb * S // ROW_DMA_TM
            y = _combine_ln(rows, ys, pos3, ln_g[i, 1][None, :], ln_b[i, 1][None, :], t0, nt)
            x3s.append(y.reshape(nb, S, D))
            t0 += nt
    return x3s


def kernel(x_prompt, x_sample, t5_table, w_qkv_a, w_o_a, w_qkv_b, w_o_b, rpb_b, ln_g, ln_b,
           w_rg, b_rg, w_re, b_re, w1, w3, w2):
    y_prompt, y_sample = _trunk([x_prompt, x_sample], t5_table, w_qkv_a, w_o_a, w_qkv_b, w_o_b, rpb_b,
                                ln_g, ln_b, w_rg, b_rg, w_re, b_re, w1, w3, w2)
    return (y_prompt, y_sample)
```

```python
import functools
import math

import numpy as np
import jax
import jax.numpy as jnp
from jax import lax
from jax.experimental import pallas as pl
from jax.experimental.pallas import tpu as pltpu

D_MODEL = 1024
SEQ = 4096
DEPTH = 4
HEAD_DIM = 64
N_HEADS = 16
DILATIONS = ((128, 1), (512, 4), (2048, 16))
N_BUCKETS = 32
T5_MAX_DISTANCE = 1024
GRID_W = 64
NA_KH = 8
NA_KW = 16
N_GROUPS = 4
EXPERTS_PER_GROUP = 8
N_EXPERTS = N_GROUPS * EXPERTS_PER_GROUP
D_EXPERT = 512
ALPHA = (2 * DEPTH) ** 0.25
LN_EPS = 1e-5
NEG_INF = -1e30

LANES = 128
SUBLANES = 8
HEAD_PAIRS = N_HEADS * HEAD_DIM // LANES
QKV_BLOCKS = 3 * HEAD_PAIRS
VMEM_LIMIT = 56 * 1024 * 1024

QBLK = 128
KBLK = 256
HALF = 64
NA_ROWS_PER_ITER = 64
DIL_BLOCKS_PER_ITER = 32
PAIRS_PER_GROUP = EXPERTS_PER_GROUP * (EXPERTS_PER_GROUP - 1) // 2
N_CLASSES = N_GROUPS * PAIRS_PER_GROUP
ROW_W = D_MODEL + LANES
MOE_TM = 256
PLAN_BLOCK = 512
TOK_TM = 1024
ROW_DMA_TM = 1024
DMA_UNROLL = 8

f32 = jnp.float32
bf16 = jnp.bfloat16


def _params(n_grid):
    return pltpu.CompilerParams(dimension_semantics=("arbitrary",) * n_grid,
                                vmem_limit_bytes=VMEM_LIMIT)


def _select_source(refs, split, read):
    if len(refs) == 1:
        return read(refs[0])
    return jnp.where(pl.program_id(0) < split, read(refs[0]), read(refs[1]))


def _source_index(k, n_src, split):
    if n_src == 1:
        return lambda b: b
    return (lambda b: jnp.minimum(b, split - 1)) if k == 0 else (lambda b: jnp.maximum(b - split, 0))


def _qkv_kernel(*refs, dil, tm, cw, n_src, split):
    x_refs = refs[:n_src]
    w_ref, o_ref, acc_ref = refs[n_src:]
    nsub = cw // LANES
    rows = tm // dil
    rows_in = len(x_refs[0].shape) == 3
    if rows_in:
        xb = jnp.concatenate([_select_source(x_refs, split, lambda ref, r=r: ref[:, r, :]) for r in range(dil)],
                             axis=0).astype(bf16)
    else:
        xb = _select_source(x_refs, split, lambda ref: ref[...]).astype(bf16)
    for j in range(QKV_BLOCKS // nsub):
        res = jnp.dot(xb, w_ref[:, j * cw:(j + 1) * cw], preferred_element_type=f32)
        if dil == 1 or rows_in:
            for kk in range(nsub):
                for r in range(dil):
                    o_ref[j * nsub + kk, r, :, :] = res[r * rows:(r + 1) * rows,
                                                        kk * LANES:(kk + 1) * LANES].astype(bf16)
            continue
        for kk in range(nsub):
            acc_ref[kk] = res[:, kk * LANES:(kk + 1) * LANES]
        for kk in range(nsub):
            for r in range(dil):
                o_ref[j * nsub + kk, r, :, :] = acc_ref[kk, pl.ds(r, rows, stride=dil), :].astype(bf16)


def _qkv_proj(xs, w, dil):
    _, S, D = xs[0].shape
    n_src, split = len(xs), xs[0].shape[0]
    B = sum(x.shape[0] for x in xs)
    tm, cw = 1024, 512
    L = S // dil
    in_specs, args = [], []
    for k, x3 in enumerate(xs):
        src = _source_index(k, n_src, split)
        if dil % SUBLANES == 0:
            x3 = x3.reshape(x3.shape[0], L, dil, D)
            in_specs.append(pl.BlockSpec((None, tm // dil, dil, D), lambda b, i, src=src: (src(b), i, 0, 0)))
        else:
            in_specs.append(pl.BlockSpec((None, tm, D), lambda b, i, src=src: (src(b), i, 0)))
        args.append(x3)
    in_specs.append(pl.BlockSpec((D, QKV_BLOCKS * LANES), lambda b, i: (0, 0)))
    return pl.pallas_call(
        functools.partial(_qkv_kernel, dil=dil, tm=tm, cw=cw, n_src=n_src, split=split),
        grid=(B, S // tm),
        in_specs=in_specs,
        out_specs=pl.BlockSpec((None, QKV_BLOCKS, dil, tm // dil, LANES), lambda b, i: (b, 0, 0, i, 0)),
        out_shape=jax.ShapeDtypeStruct((B, QKV_BLOCKS, dil, L, LANES), bf16),
        scratch_shapes=[pltpu.VMEM((cw // LANES, tm, LANES), f32)],
        compiler_params=_params(2),
        name=f"qkv_proj_d{dil}",
    )(*args, w)


def _softmax_block(qst, kb, vb, bias):
    nq = qst.shape[0] // 2
    s = lax.dot_general(qst, kb, (((1,), (1,)), ((), ())), preferred_element_type=f32) + bias
    m = jnp.max(s, axis=1, keepdims=True)
    p = jnp.exp((s - m).astype(bf16))
    v1 = jnp.concatenate([vb, jnp.ones_like(vb)], axis=1)
    ol = jnp.dot(p, v1, preferred_element_type=f32)
    l = ol[:, LANES:]
    o2 = ol[:, :LANES] * (1.0 / l)
    lse = m + jnp.log(l)
    lane = lax.broadcasted_iota(jnp.int32, (nq, LANES), 1)
    o = jnp.where(lane < HEAD_DIM, o2[:nq], o2[nq:])
    ls = jnp.where(lane < HEAD_DIM, lse[:nq], lse[nq:])
    return o, ls


def _stack_heads(q2):
    lane = lax.broadcasted_iota(jnp.int32, q2.shape, 1)
    zero = jnp.zeros_like(q2)
    return jnp.concatenate([jnp.where(lane < HEAD_DIM, q2, zero),
                            jnp.where(lane >= HEAD_DIM, q2, zero)], axis=0)


def _attn_a_kernel(*refs):
    qkv = refs[:9]
    biases = refs[9:12]
    o_ref = refs[12]
    oscr, lscr = refs[13], refs[14]
    S = o_ref.shape[0]

    def block(it, g):
        dil = DILATIONS[g][1]
        q_ref, k_ref, v_ref = qkv[3 * g:3 * g + 3]
        L = S // dil
        nb = L // QBLK
        r = it // nb
        n = it % nb
        qs = pl.multiple_of(n * QBLK, QBLK)
        ks = pl.multiple_of(jnp.clip(qs - HALF, 0, L - KBLK), HALF)
        var = jnp.where(n == 0, 0, jnp.where(n == nb - 1, 2, 1))
        qst = _stack_heads(q_ref[r, pl.ds(qs, QBLK), :])
        o, ls = _softmax_block(qst, k_ref[r, pl.ds(ks, KBLK), :], v_ref[r, pl.ds(ks, KBLK), :], biases[g][var])
        if dil == 1:
            dst = pl.ds(qs, QBLK)
        else:
            dst = pl.ds(qs * dil + r, QBLK, stride=dil)
        oscr[g, dst, :] = o
        lscr[g, dst, :] = ls

    def body(it, carry):
        for u in range(DIL_BLOCKS_PER_ITER):
            for g in range(len(DILATIONS)):
                block(it * DIL_BLOCKS_PER_ITER + u, g)
        return carry

    lax.fori_loop(0, S // QBLK // DIL_BLOCKS_PER_ITER, body, 0)

    ch = 256

    def merge(c, carry):
        rows = pl.ds(pl.multiple_of(c * ch, ch), ch)
        l0, l1, l2 = lscr[0, rows, :], lscr[1, rows, :], lscr[2, rows, :]
        mx = jnp.maximum(jnp.maximum(l0, l1), l2)
        e0, e1, e2 = jnp.exp(l0 - mx), jnp.exp(l1 - mx), jnp.exp(l2 - mx)
        inv = 1.0 / (e0 + e1 + e2)
        o = (e0 * oscr[0, rows, :] + e1 * oscr[1, rows, :] + e2 * oscr[2, rows, :]) * inv
        o_ref[rows, :] = o.astype(bf16)
        return carry

    lax.fori_loop(0, S // ch, merge, 0)


def _attn_a(qkvs, biases):
    B = qkvs[0].shape[0]
    S = qkvs[0].shape[2] * qkvs[0].shape[3]
    in_specs, args = [], []
    for g, (_, dil) in enumerate(DILATIONS):
        L = S // dil
        for c in range(3):
            in_specs.append(pl.BlockSpec((None, None, dil, L, LANES),
                                         lambda b, hp, c=c: (b, c * HEAD_PAIRS + hp, 0, 0, 0)))
            args.append(qkvs[g])
    for g in range(3):
        in_specs.append(pl.BlockSpec((None, 3, 2 * QBLK, KBLK), lambda b, hp: (hp, 0, 0, 0)))
        args.append(biases[g])
    return pl.pallas_call(
        _attn_a_kernel,
        grid=(B, HEAD_PAIRS),
        in_specs=in_specs,
        out_specs=pl.BlockSpec((None, None, S, LANES), lambda b, hp: (b, hp, 0, 0)),
        out_shape=jax.ShapeDtypeStruct((B, HEAD_PAIRS, S, LANES), bf16),
        scratch_shapes=[pltpu.VMEM((3, S, LANES), f32), pltpu.VMEM((3, S, LANES), f32)],
        compiler_params=_params(2),
        name="dilated_attn",
    )(*args)


def _attn_b_kernel(q_ref, k_ref, v_ref, b_ref, o_ref):
    rows = o_ref.shape[0] // GRID_W
    kh = min(NA_KH, rows)
    nk = kh * GRID_W

    def one_row(i):
        rs = jnp.clip(i - kh // 2, 0, rows - kh)
        var = rs - i + NA_KH - 1 - (NA_KH - kh)
        qs = pl.multiple_of(i * GRID_W, GRID_W)
        ks = pl.multiple_of(rs * GRID_W, GRID_W)
        qst = _stack_heads(q_ref[pl.ds(qs, GRID_W), :])
        o, _ = _softmax_block(qst, k_ref[pl.ds(ks, nk), :], v_ref[pl.ds(ks, nk), :], b_ref[var])
        o_ref[pl.ds(qs, GRID_W), :] = o.astype(bf16)

    def body(it, carry):
        for u in range(NA_ROWS_PER_ITER):
            one_row(it * NA_ROWS_PER_ITER + u)
        return carry

    lax.fori_loop(0, rows // NA_ROWS_PER_ITER, body, 0)


def _attn_b(qkv, bias):
    B, _, _, S, _ = qkv.shape
    nvar, nk = bias.shape[1], bias.shape[3]
    in_specs = [pl.BlockSpec((None, None, None, S, LANES), lambda b, hp, c=c: (b, c * HEAD_PAIRS + hp, 0, 0, 0))
                for c in range(3)]
    in_specs.append(pl.BlockSpec((None, nvar, 2 * GRID_W, nk), lambda b, hp: (hp, 0, 0, 0)))
    return pl.pallas_call(
        _attn_b_kernel,
        grid=(B, HEAD_PAIRS),
        in_specs=in_specs,
        out_specs=pl.BlockSpec((None, None, S, LANES), lambda b, hp: (b, hp, 0, 0)),
        out_shape=jax.ShapeDtypeStruct((B, HEAD_PAIRS, S, LANES), bf16),
        compiler_params=_params(2),
        name="neighbourhood_attn",
    )(qkv, qkv, qkv, bias)


def _layer_norm(y, g, b):
    mu = jnp.mean(y, axis=1, keepdims=True)
    yc = y - mu
    var = jnp.mean(yc * yc, axis=1, keepdims=True)
    return yc * lax.rsqrt(var + LN_EPS) * g + b


def _route(logits):
    shape = logits.shape
    lane = lax.broadcasted_iota(jnp.int32, shape, 1)
    big = jnp.int32(4 * LANES)
    is_g = lane < N_GROUPS
    gl = jnp.where(is_g, logits, NEG_INF)
    mg = jnp.max(gl, axis=1, keepdims=True)
    gsel = jnp.min(jnp.where(gl == mg, lane, big), axis=1, keepdims=True)
    p_sel = 1.0 / jnp.sum(jnp.where(is_g, jnp.exp(gl - mg), 0.0), axis=1, keepdims=True)
    lo = N_GROUPS + gsel * EXPERTS_PER_GROUP
    el = jnp.where((lane >= lo) & (lane < lo + EXPERTS_PER_GROUP), logits, NEG_INF)
    v1 = jnp.max(el, axis=1, keepdims=True)
    i1 = jnp.min(jnp.where(el == v1, lane, big), axis=1, keepdims=True)
    el2 = jnp.where(lane == i1, NEG_INF, el)
    v2 = jnp.max(el2, axis=1, keepdims=True)
    i2 = jnp.min(jnp.where(el2 == v2, lane, big), axis=1, keepdims=True)
    t = jnp.exp(v2 - v1)
    g1 = p_sel / (1.0 + t)
    g2 = p_sel * t / (1.0 + t)
    first_lo = i1 < i2
    a = jnp.where(first_lo, i1, i2) - lo
    b = jnp.where(first_lo, i2, i1) - lo
    ga = jnp.where(first_lo, g1, g2)
    gb = jnp.where(first_lo, g2, g1)
    pair = (EXPERTS_PER_GROUP - 1) * a - ((a * (a - 1)) >> 1) + (b - a - 1)
    cls = (gsel * PAIRS_PER_GROUP + pair).astype(f32)
    return jnp.where(lane == 0, ga, jnp.where(lane == 1, gb, jnp.where(lane == 2, cls, 0.0)))


def _oproj_kernel(*refs, n_src, split):
    att_ref = refs[0]
    x_refs = refs[1:1 + n_src]
    wo_ref, g_ref, b_ref, wr_ref, br_ref, o_ref = refs[1 + n_src:]
    a = jnp.concatenate([att_ref[h] for h in range(HEAD_PAIRS)], axis=1)
    h = jnp.dot(a, wo_ref[...], preferred_element_type=f32)
    x = _select_source(x_refs, split, lambda ref: ref[...])
    xn = _layer_norm(ALPHA * x + h, g_ref[...], b_ref[...])
    xh = xn.astype(bf16)
    xl = (xn - xh.astype(f32)).astype(bf16)
    hw = jnp.dot(xh, wr_ref[...], preferred_element_type=f32)
    lw = jnp.dot(xl, wr_ref[:, :LANES], preferred_element_type=f32)
    logits = hw[:, :LANES] + hw[:, LANES:] + lw + br_ref[...]
    o_ref[:, :D_MODEL] = xn
    o_ref[:, D_MODEL:] = _route(logits)


def _oproj_ln_route(att, xs, wo, g, b, wr, br):
    B, _, S, _ = att.shape
    T = B * S
    tm = TOK_TM
    per = S // tm
    n_src, split = len(xs), xs[0].shape[0] // tm
    vec = lambda n: pl.BlockSpec((1, n), lambda i: (0, 0))
    x_specs = [pl.BlockSpec((tm, D_MODEL), lambda i, src=_source_index(k, n_src, split): (src(i), 0))
               for k in range(n_src)]
    return pl.pallas_call(
        functools.partial(_oproj_kernel, n_src=n_src, split=split),
        grid=(T // tm,),
        in_specs=[pl.BlockSpec((None, HEAD_PAIRS, tm, LANES), lambda i: (i // per, 0, i % per, 0))] + x_specs
                 + [pl.BlockSpec((D_MODEL, D_MODEL), lambda i: (0, 0)), vec(D_MODEL), vec(D_MODEL),
                    pl.BlockSpec((D_MODEL, 2 * LANES), lambda i: (0, 0)), vec(LANES)],
        out_specs=pl.BlockSpec((tm, ROW_W), lambda i: (i, 0)),
        out_shape=jax.ShapeDtypeStruct((T, ROW_W), f32),
        compiler_params=_params(1),
        name="oproj_ln_route",
    )(att, *xs, wo, g, b, wr, br)


def _for_rows(n, fn):
    def body(c, carry):
        base = pl.multiple_of(c * DMA_UNROLL, DMA_UNROLL)
        for u in range(DMA_UNROLL):
            fn(base, u)
        return carry

    lax.fori_loop(0, n // DMA_UNROLL, body, 0)


def _tile_row(ref, base, u):
    return ref.at[pl.ds(base, DMA_UNROLL), :].at[pl.ds(u, 1), :]


def _dispatch_kernel(pos_ref, rows_ref, init_ref, xs_ref, sem):
    del init_ref

    def copy(base, u):
        dst = xs_ref.at[pl.ds(pos_ref[0, 0, base + u], 1), :]
        return pltpu.make_async_copy(_tile_row(rows_ref, base, u), dst, sem)

    n = rows_ref.shape[0]
    _for_rows(n, lambda base, u: copy(base, u).start(priority=u % 2))
    _for_rows(n, lambda base, u: copy(base, u).wait())


def _dispatch(rows, pos3, slots):
    T = rows.shape[0]
    tm = pos3.shape[2]
    return pl.pallas_call(
        _dispatch_kernel,
        grid=(T // tm,),
        in_specs=[pl.BlockSpec((1, 1, tm), lambda i: (i, 0, 0), memory_space=pltpu.SMEM),
                  pl.BlockSpec((tm, ROW_W), lambda i: (i, 0)),
                  pl.BlockSpec(memory_space=pl.ANY)],
        out_specs=pl.BlockSpec(memory_space=pl.ANY),
        out_shape=jax.ShapeDtypeStruct(slots.shape, f32),
        scratch_shapes=[pltpu.SemaphoreType.DMA(())],
        input_output_aliases={2: 0},
        compiler_params=_params(1),
        name="moe_dispatch",
    )(pos3, rows, slots)


def _swiglu(xt, w1, w3, w2):
    h1 = jnp.dot(xt, w1, preferred_element_type=f32)
    h3 = jnp.dot(xt, w3, preferred_element_type=f32)
    h = h1 * (1.0 / (1.0 + jnp.exp(-h1))) * h3
    return jnp.dot(h.astype(bf16), w2, preferred_element_type=f32)


def _moe_kernel(grp_ref, la_ref, lb_ref, nrows_ref, xs_ref, w1_ref, w3_ref, w2_ref, ys_ref):
    del grp_ref
    i = pl.program_id(0)
    tm = xs_ref.shape[0]
    half = tm // 2

    def experts(rows):
        a, b = la_ref[i], lb_ref[i]
        xt = xs_ref[rows, :D_MODEL].astype(bf16)
        ga = xs_ref[rows, D_MODEL:D_MODEL + 1]
        gb = xs_ref[rows, D_MODEL + 1:D_MODEL + 2]
        ys_ref[rows, :] = (ga * _swiglu(xt, w1_ref[a], w3_ref[a], w2_ref[a])
                           + gb * _swiglu(xt, w1_ref[b], w3_ref[b], w2_ref[b]))

    @pl.when(nrows_ref[i] > half)
    def _():
        experts(slice(0, tm))

    @pl.when((nrows_ref[i] > 0) & (nrows_ref[i] <= half))
    def _():
        experts(slice(0, half))
        ys_ref[half:, :] = jnp.zeros((tm - half, D_MODEL), f32)

    @pl.when(nrows_ref[i] == 0)
    def _():
        ys_ref[...] = jnp.zeros_like(ys_ref)


def _moe(xs, tile_grp, tile_a, tile_b, tile_rows, w1, w3, w2, layer):
    n_slots = xs.shape[0]
    tm = MOE_TM
    group_w = lambda shape: pl.BlockSpec((None, None, EXPERTS_PER_GROUP) + shape,
                                         lambda i, g, a, b, va: (layer, g[i], 0, 0, 0),
                                         pipeline_mode=pl.Buffered(1))
    up, down = (D_MODEL, D_EXPERT), (D_EXPERT, D_MODEL)
    grid_spec = pltpu.PrefetchScalarGridSpec(
        num_scalar_prefetch=4,
        grid=(n_slots // tm,),
        in_specs=[pl.BlockSpec((tm, ROW_W), lambda i, g, a, b, va: (i, 0)),
                  group_w(up), group_w(up), group_w(down)],
        out_specs=pl.BlockSpec((tm, D_MODEL), lambda i, g, a, b, va: (i, 0)),
    )
    return pl.pallas_call(
        _moe_kernel,
        grid_spec=grid_spec,
        out_shape=jax.ShapeDtypeStruct((n_slots, D_MODEL), f32),
        compiler_params=_params(1),
        name="moe_experts",
    )(tile_grp, tile_a, tile_b, tile_rows, xs, w1, w3, w2)


def _combine_kernel(pos_ref, pos_next_ref, x_ref, ys_ref, g_ref, b_ref, o_ref, buf, sem):
    i = pl.program_id(0)
    n = x_ref.shape[0]
    slot = i % 2

    def copy(p_ref, s):
        def make(base, u):
            src = ys_ref.at[pl.ds(p_ref[0, 0, base + u], 1), :]
            return pltpu.make_async_copy(src, _tile_row(buf.at[s], base, u), sem.at[s])
        return make

    @pl.when(i == 0)
    def _():
        _for_rows(n, lambda base, u: copy(pos_ref, slot)(base, u).start(priority=u % 2))

    @pl.when(i + 1 < pl.num_programs(0))
    def _():
        _for_rows(n, lambda base, u: copy(pos_next_ref, 1 - slot)(base, u).start(priority=u % 2))

    _for_rows(n, lambda base, u: copy(pos_ref, slot)(base, u).wait())
    o_ref[...] = _layer_norm(ALPHA * x_ref[...] + buf[slot], g_ref[...], b_ref[...])


def _combine_ln(rows, ys, pos3, g, b, t0, nt):
    tm = pos3.shape[2]
    vec = pl.BlockSpec((1, D_MODEL), lambda i: (0, 0))
    return pl.pallas_call(
        _combine_kernel,
        grid=(nt,),
        in_specs=[pl.BlockSpec((1, 1, tm), lambda i: (i + t0, 0, 0), memory_space=pltpu.SMEM),
                  pl.BlockSpec((1, 1, tm), lambda i: (jnp.minimum(i + 1, nt - 1) + t0, 0, 0),
                               memory_space=pltpu.SMEM),
                  pl.BlockSpec((tm, D_MODEL), lambda i: (i + t0, 0)),
                  pl.BlockSpec(memory_space=pl.ANY), vec, vec],
        out_specs=pl.BlockSpec((tm, D_MODEL), lambda i: (i, 0)),
        out_shape=jax.ShapeDtypeStruct((nt * tm, D_MODEL), f32),
        scratch_shapes=[pltpu.VMEM((2, tm, D_MODEL), f32), pltpu.SemaphoreType.DMA((2,))],
        compiler_params=_params(1),
        name="moe_combine_ln",
    )(pos3, pos3, rows, ys, g, b)


def _plan(cls, n_tiles):
    T = cls.shape[0]
    blk = PLAN_BLOCK
    onehot = cls[:, None] == jnp.arange(N_CLASSES, dtype=jnp.int32)[None, :]
    oh3 = onehot.astype(bf16).reshape(T // blk, blk, N_CLASSES)
    tri = (np.arange(blk)[:, None] >= np.arange(blk)[None, :]).astype(np.float32)
    within = jnp.einsum("ij,bjc->bic", jnp.asarray(tri, bf16), oh3, preferred_element_type=f32)
    blk_tot = within[:, -1, :]
    blk_end = jnp.cumsum(blk_tot, axis=0)
    csum = (within + (blk_end - blk_tot)[:, None, :]).reshape(T, N_CLASSES)
    rank = jnp.sum(jnp.where(onehot, csum, 0.0), axis=1).astype(jnp.int32) - 1
    counts = blk_end[-1].astype(jnp.int32)
    tiles = (counts + MOE_TM - 1) // MOE_TM
    tile_end = jnp.cumsum(tiles)
    tile_start = tile_end - tiles
    pos = jnp.sum(jnp.where(onehot, tile_start[None, :], 0), axis=1) * MOE_TM + rank
    tile_id = jnp.arange(n_tiles, dtype=jnp.int32)
    used = tile_end[-1]
    tcls = jnp.sum(jnp.minimum(tile_id, used - 1)[:, None] >= tile_end[None, :], axis=1).astype(jnp.int32)
    tile_onehot = tcls[:, None] == jnp.arange(N_CLASSES, dtype=jnp.int32)[None, :]
    lookup = lambda table: jnp.sum(jnp.where(tile_onehot, jnp.asarray(table, jnp.int32)[None, :], 0), axis=1)
    pa, pb = np.triu_indices(EXPERTS_PER_GROUP, 1)
    grp = tcls // PAIRS_PER_GROUP
    la = lookup(np.tile(pa, N_GROUPS))
    lb = lookup(np.tile(pb, N_GROUPS))
    nrows = jnp.clip(lookup(counts) - (tile_id - lookup(tile_start)) * MOE_TM, 0, MOE_TM)
    nrows = jnp.where(tile_id < used, nrows, 0).astype(jnp.int32)
    return pos.astype(jnp.int32), grp, la, lb, nrows


def _t5_buckets(rel):
    nb = N_BUCKETS // 2
    max_exact = nb // 2
    ret = np.where(rel > 0, nb, 0)
    n = np.abs(rel)
    nf = np.maximum(n, 1).astype(np.float32)
    large = max_exact + (np.log(nf / np.float32(max_exact)) / np.float32(math.log(T5_MAX_DISTANCE / max_exact))
                         * np.float32(nb - max_exact)).astype(np.int32)
    large = np.minimum(large, nb - 1)
    return ret + np.where(n < max_exact, n, large)


def _table_lookup(table, idx):
    onehot = (jnp.asarray(idx, jnp.int32)[..., None] == jnp.arange(table.shape[0], dtype=jnp.int32)).astype(f32)
    return jnp.einsum("...n,nh->...h", onehot, table, precision=lax.Precision.HIGHEST)


def _dilated_bias(t5_table, dil):
    qi = np.arange(QBLK)[:, None]
    kj = np.arange(KBLK)[None, :]
    out = []
    for shift in (0, HALF, 2 * HALF):
        rel = kj - qi - shift
        band = np.abs(rel) <= HALF
        vals = _table_lookup(t5_table.astype(f32), _t5_buckets(rel * dil))
        vals = jnp.where(jnp.asarray(band)[:, :, None], vals, NEG_INF)
        out.append(vals.transpose(2, 0, 1))
    tab = jnp.stack(out, axis=1)
    tab = tab.reshape(HEAD_PAIRS, 2, 3, QBLK, KBLK).transpose(0, 2, 1, 3, 4)
    return tab.reshape(HEAD_PAIRS, 3, 2 * QBLK, KBLK)


def _neighbourhood_bias(rpb, rows):
    kh = min(NA_KH, rows)
    nvar = NA_KH if rows > kh else 1
    j = np.arange(GRID_W)[:, None]
    kc = np.arange(GRID_W)[None, :]
    ws = np.clip(j - NA_KW // 2, 0, GRID_W - NA_KW)
    inwin = (kc >= ws) & (kc < ws + NA_KW)
    dc = np.clip(kc - j + NA_KW - 1, 0, 2 * NA_KW - 2)
    ndr, ndc = rpb.shape[1], rpb.shape[2]
    cols = _table_lookup(rpb.astype(f32).reshape(N_HEADS * ndr, ndc).T, dc)
    cols = jnp.where(jnp.asarray(inwin)[:, :, None], cols, NEG_INF)
    cols = cols.reshape(GRID_W, GRID_W, N_HEADS, ndr).transpose(2, 3, 0, 1)
    out = []
    for var in range(nvar):
        lo = var + (NA_KH - kh)
        vals = cols[:, lo:lo + kh]
        out.append(vals.transpose(0, 2, 1, 3).reshape(N_HEADS, GRID_W, kh * GRID_W))
    tab = jnp.stack(out, axis=1)
    tab = tab.reshape(HEAD_PAIRS, 2, nvar, GRID_W, kh * GRID_W).transpose(0, 2, 1, 3, 4)
    return tab.reshape(HEAD_PAIRS, nvar, 2 * GRID_W, kh * GRID_W)


def _qkv_weight(w):
    hd = N_HEADS * HEAD_DIM
    scale = jnp.concatenate([jnp.full((hd,), HEAD_DIM ** -0.5, f32), jnp.ones((2 * hd,), f32)])
    return (w * scale[None, :]).astype(bf16)


def _router_weight(w_rg, b_rg, w_re, b_re):
    pad = LANES - N_GROUPS - N_EXPERTS
    wr = jnp.concatenate([w_rg, w_re, jnp.zeros((D_MODEL, pad), f32)], axis=1)
    br = jnp.concatenate([b_rg, b_re, jnp.zeros((pad,), f32)])[None, :]
    hi = wr.astype(bf16)
    lo = (wr - hi.astype(f32)).astype(bf16)
    return jnp.concatenate([hi, lo], axis=1), br


def _trunk(x_parts, t5_table, w_qkv_a, w_o_a, w_qkv_b, w_o_b, rpb_b, ln_g, ln_b,
           w_rg, b_rg, w_re, b_re, w1, w3, w2):
    _, S, D = x_parts[0].shape
    batches = [x.shape[0] for x in x_parts]
    B = sum(batches)
    T = B * S
    n_slots = T + N_CLASSES * MOE_TM
    n_tiles = n_slots // MOE_TM
    hd = N_HEADS * HEAD_DIM
    by_group = lambda w: w.astype(bf16).reshape(DEPTH, N_GROUPS, EXPERTS_PER_GROUP, *w.shape[2:])
    w1b, w3b, w2b = by_group(w1), by_group(w3), by_group(w2)
    bias_a = [_dilated_bias(t5_table, dil) for _, dil in DILATIONS]
    xs = jnp.zeros((n_slots, ROW_W), f32)
    x3s = list(x_parts)
    for i in range(DEPTH):
        j = i // 2
        if i % 2 == 0:
            qkvs = [_qkv_proj(x3s, _qkv_weight(w_qkv_a[j][:, g * 3 * hd:(g + 1) * 3 * hd]), dil)
                    for g, (_, dil) in enumerate(DILATIONS)]
            att = _attn_a(qkvs, bias_a)
            wo = w_o_a[j]
        else:
            qkv = _qkv_proj(x3s, _qkv_weight(w_qkv_b[j]), 1)
            att = _attn_b(qkv, _neighbourhood_bias(rpb_b[j], S // GRID_W))
            wo = w_o_b[j]
        wr, br = _router_weight(w_rg[i], b_rg[i], w_re[i], b_re[i])
        rows = _oproj_ln_route(att, [x.reshape(-1, D) for x in x3s], wo.astype(bf16),
                               ln_g[i, 0][None, :], ln_b[i, 0][None, :], wr, br)
        cls = rows[:, D_MODEL + 2].astype(jnp.int32)
        pos, grp, la, lb, nrows = _plan(cls, n_tiles)
        pos3 = pos.reshape(T // ROW_DMA_TM, 1, ROW_DMA_TM)
        xs = _dispatch(rows, pos3, xs)
        ys = _moe(xs, grp, la, lb, nrows, w1b, w3b, w2b, i)
        out_batches = batches if i == DEPTH - 1 else [B]
        x3s, t0 = [], 0
        for nb in out_batches:
            nt = nb * S // ROW_DMA_TM
            y = _combine_ln(rows, ys, pos3, ln_g[i, 1][None, :], ln_b[i, 1][None, :], t0, nt)
            x3s.append(y.reshape(nb, S, D))
            t0 += nt
    return x3s


def kernel(x_prompt, x_sample, t5_table, w_qkv_a, w_o_a, w_qkv_b, w_o_b, rpb_b, ln_g, ln_b,
           w_rg, b_rg, w_re, b_re, w1, w3, w2):
    y_prompt, y_sample = _trunk([x_prompt, x_sample], t5_table, w_qkv_a, w_o_a, w_qkv_b, w_o_b, rpb_b,
                                ln_g, ln_b, w_rg, b_rg, w_re, b_re, w1, w3, w2)
    return (y_prompt, y_sample)
```
